```python
import math
import jax, jax.numpy as jnp
from jax import lax
import numpy as np

D_MODEL = 2048
BATCH = 1
SEQ = 16384
DEPTH = 2

CTX_LEN = 256
GRID_W = 64
N_MOD = 9
D_FF = 5632
NORM_EPS = 1e-6
MLA_HEADS = 8
QK_NOPE = 128
QK_ROPE = 64
QK_DIM = QK_NOPE + QK_ROPE
V_DIM = 128
Q_LORA = 512
KV_LORA = 256
ROPE_BASE = 10000.0
Q_BLOCK = 128
S5_WIDTH = 1024
S5_GROUP = 16
S5_GROUPS = S5_WIDTH // S5_GROUP
S5_STATE = 64
DT_MIN = 1e-3
DT_MAX = 1e-1
EV_IN = Q_LORA + KV_LORA + QK_ROPE + S5_WIDTH
EV_MIX = MLA_HEADS * V_DIM + S5_WIDTH
HY_WIDTH = D_MODEL
HY_ORDER = 2
HY_SHORT = 3
HY_BANDS = 16
HY_EMB = 1 + 2 * HY_BANDS
HY_FILT_HIDDEN = 64
HY_DECAY_MIN = math.log(1e-2) / 1.5
HY_DECAY_MAX = math.log(1e-2) / 0.3

N_EVEN = (DEPTH + 1) // 2
N_ODD = DEPTH // 2

kernel_name = "hybrid_mla_s5_hyena_prefix_dit"


def rms_norm(x, g):
    xf = x.astype(jnp.float32)
    y = xf * lax.rsqrt(jnp.mean(xf * xf, axis=-1, keepdims=True) + NORM_EPS)
    return (y * g.astype(jnp.float32)).astype(x.dtype)


def modulate(h, shift, scale):
    return h * (1 + scale) + shift


def swiglu(h, w_in, w_out):
    g, u = jnp.split(h @ w_in, 2, axis=-1)
    return (jax.nn.silu(g) * u) @ w_out


def ffn_half(s, mod, k0, g_norm, w_in, w_out):
    h = modulate(rms_norm(s, g_norm), mod[k0], mod[k0 + 1])
    return s + mod[k0 + 2] * (0.5 * swiglu(h, w_in, w_out))


def axial_rope_tables(n_rows, dtype):
    row = jnp.repeat(jnp.arange(n_rows, dtype=jnp.float32), GRID_W)
    col = jnp.tile(jnp.arange(GRID_W, dtype=jnp.float32), n_rows)
    n_freq = QK_ROPE // 4
    inv_freq = ROPE_BASE ** (-jnp.arange(n_freq, dtype=jnp.float32) / n_freq)
    ang_r = row[:, None] * inv_freq
    ang_c = col[:, None] * inv_freq
    return tuple(t[:, None, :].astype(dtype) for t in
                 (jnp.cos(ang_r), jnp.sin(ang_r), jnp.cos(ang_c), jnp.sin(ang_c)))


def _rotate_pairs(x, cos, sin):
    x1, x2 = jnp.split(x, 2, axis=-1)
    return jnp.concatenate([x1 * cos - x2 * sin, x2 * cos + x1 * sin], axis=-1)


def rope_tail(t, rope):
    if rope is None:
        return t
    cos_r, sin_r, cos_c, sin_c = rope
    t_nope, t_pe = jnp.split(t, [QK_NOPE], axis=-1)
    pe_r, pe_c = jnp.split(t_pe, 2, axis=-1)
    return jnp.concatenate([t_nope, _rotate_pairs(pe_r, cos_r, sin_r), _rotate_pairs(pe_c, cos_c, sin_c)], axis=-1)


def mla_heads(q_a, kv_a, k_pe, q_a_norm_g, w_uq, kv_a_norm_g, w_ukv, q_head_g, k_head_g, rope, with_queries):
    b, n, _ = kv_a.shape
    kv = (rms_norm(kv_a, kv_a_norm_g) @ w_ukv).reshape(b, n, MLA_HEADS, QK_NOPE + V_DIM)
    k_nope, v = jnp.split(kv, [QK_NOPE], axis=-1)
    k_rot = jnp.broadcast_to(k_pe[:, :, None, :], (b, n, MLA_HEADS, QK_ROPE))
    k = rope_tail(rms_norm(jnp.concatenate([k_nope, k_rot], axis=-1), k_head_g), rope)
    if not with_queries:
        return None, k, v
    q = (rms_norm(q_a, q_a_norm_g) @ w_uq).reshape(b, n, MLA_HEADS, QK_DIM)
    q = rope_tail(rms_norm(q, q_head_g), rope)
    return q, k, v


def block_attention(q, k, v):
    b, n, h, d = q.shape
    scale = 1.0 / math.sqrt(QK_DIM)
    qb = q.reshape(b, n // Q_BLOCK, Q_BLOCK, h, d).transpose(1, 0, 2, 3, 4)

    def one_block(q_blk):
        s = jnp.einsum('bqhd,bkhd->bhqk', q_blk, k, preferred_element_type=jnp.float32) * scale
        p = jax.nn.softmax(s, axis=-1).astype(v.dtype)
        return jnp.einsum('bhqk,bkhe->bqhe', p, v)

    o = lax.map(one_block, qb)
    return o.transpose(1, 0, 2, 3, 4).reshape(b, n, h * V_DIM)


def s5_discretize(lam_re, lam_im, log_dt):
    dt = jnp.exp(log_dt)[:, None]
    mag = jnp.exp(lam_re * dt)
    a_re = mag * jnp.cos(lam_im * dt)
    a_im = mag * jnp.sin(lam_im * dt)
    den = lam_re * lam_re + lam_im * lam_im
    nr = a_re - 1.0
    k_re = (nr * lam_re + a_im * lam_im) / den
    k_im = (a_im * lam_re - nr * lam_im) / den
    return a_re, a_im, k_re, k_im


def _complex_affine_combine(e1, e2):
    a1r, a1i, b1r, b1i = e1
    a2r, a2i, b2r, b2i = e2
    return (a2r * a1r - a2i * a1i, a2r * a1i + a2i * a1r,
            a2r * b1r - a2i * b1i + b2r, a2r * b1i + a2i * b1r + b2i)


def s5_scan(u, lam_re, lam_im, log_dt, b_re, b_im, h0, reverse):
    a_re, a_im, k_re, k_im = s5_discretize(lam_re, lam_im, log_dt)
    bu_re = jnp.einsum('blgc,gpc->blgp', u, b_re)
    bu_im = jnp.einsum('blgc,gpc->blgp', u, b_im)
    x_re = k_re * bu_re - k_im * bu_im
    x_im = k_re * bu_im + k_im * bu_re
    elems = (jnp.broadcast_to(a_re, x_re.shape), jnp.broadcast_to(a_im, x_re.shape), x_re, x_im)
    p_re, p_im, h_re, h_im = lax.associative_scan(_complex_affine_combine, elems, reverse=reverse, axis=1)
    h0_re, h0_im = h0[0][:, None], h0[1][:, None]
    return h_re + p_re * h0_re - p_im * h0_im, h_im + p_re * h0_im + p_im * h0_re


def s5_readout(h_re, h_im, c_re, c_im):
    return jnp.einsum('blgp,gcp->blgc', h_re, c_re) - jnp.einsum('blgp,gcp->blgc', h_im, c_im)


def s5_glu(y, glu_w, glu_b, dtype):
    y = jax.nn.gelu(y.reshape(y.shape[0], y.shape[1], S5_WIDTH))
    return (y * jax.nn.sigmoid(y @ glu_w.astype(jnp.float32) + glu_b.astype(jnp.float32))).astype(dtype)


def s5_mixer(u_x, u_c, lam_re, lam_im, log_dt, b_re, b_im, c_re, c_im, d_skip, glu_w, glu_b, ctx_out):
    f32 = jnp.float32
    bsz = u_x.shape[0]
    gx = u_x.astype(f32).reshape(bsz, -1, S5_GROUPS, S5_GROUP)
    gc = u_c.astype(f32).reshape(bsz, -1, S5_GROUPS, S5_GROUP)
    d_g = d_skip.astype(f32).reshape(S5_GROUPS, S5_GROUP)
    zero = jnp.zeros((bsz, S5_GROUPS, S5_STATE), f32)
    y_x = d_g * gx
    y_c = d_g * gc if ctx_out else None
    for direction, reverse in ((0, False), (1, True)):
        prm = [p[direction].astype(f32) for p in (lam_re, lam_im, log_dt, b_re, b_im)]
        cr, ci = c_re[direction].astype(f32), c_im[direction].astype(f32)
        hc_re, hc_im = s5_scan(gc, *prm, (zero, zero), reverse)
        last = 0 if reverse else -1
        hx_re, hx_im = s5_scan(gx, *prm, (hc_re[:, last], hc_im[:, last]), reverse)
        y_x = y_x + s5_readout(hx_re, hx_im, cr, ci)
        if ctx_out:
            y_c = y_c + s5_readout(hc_re, hc_im, cr, ci)
    out_x = s5_glu(y_x, glu_w, glu_b, u_x.dtype)
    out_c = s5_glu(y_c, glu_w, glu_b, u_c.dtype) if ctx_out else None
    return out_x, out_c


def even_mixer(hx, hc, w_in, q_a_norm_g, w_uq, kv_a_norm_g, w_ukv, q_head_g, k_head_g,
               lam_re, lam_im, log_dt, b_re, b_im, c_re, c_im, d_skip, glu_w, glu_b, w_out, rope, ctx_out):
    cuts = [Q_LORA, Q_LORA + KV_LORA, Q_LORA + KV_LORA + QK_ROPE]
    qa_x, kva_x, kpe_x, u_x = jnp.split(hx @ w_in, cuts, axis=-1)
    qa_c, kva_c, kpe_c, u_c = jnp.split(hc @ w_in, cuts, axis=-1)
    q_x, k_x, v_x = mla_heads(qa_x, kva_x, kpe_x, q_a_norm_g, w_uq, kv_a_norm_g, w_ukv,
                              q_head_g, k_head_g, rope, True)
    q_c, k_c, v_c = mla_heads(qa_c, kva_c, kpe_c, q_a_norm_g, w_uq, kv_a_norm_g, w_ukv,
                              q_head_g, k_head_g, None, ctx_out)
    att_x = block_attention(q_x, jnp.concatenate([k_x, k_c], axis=1), jnp.concatenate([v_x, v_c], axis=1))
    ssm_x, ssm_c = s5_mixer(u_x, u_c, lam_re, lam_im, log_dt, b_re, b_im, c_re, c_im,
                            d_skip, glu_w, glu_b, ctx_out)
    out_x = jnp.concatenate([att_x, ssm_x], axis=-1) @ w_out
    if not ctx_out:
        return out_x, None
    att_c = block_attention(q_c, k_c, v_c)
    out_c = jnp.concatenate([att_c, ssm_c], axis=-1) @ w_out
    return out_x, out_c


def hyena_filter_hidden(n, w1, b1, w2, b2, w3, b3, freq):
    f32 = jnp.float32
    t = jnp.arange(n, dtype=f32) / n
    bands = jnp.linspace(1e-4, HY_BANDS - 1, HY_BANDS, dtype=f32)
    ang = (2.0 * math.pi) * t[:, None] * bands[None, :]
    feat = jnp.concatenate([t[:, None], jnp.cos(ang), -jnp.sin(ang)], axis=-1)
    fr = freq.astype(f32)
    h = jnp.sin(fr * (feat @ w1.astype(f32) + b1.astype(f32)))
    h = jnp.sin(fr * (h @ w2.astype(f32) + b2.astype(f32)))
    h = jnp.sin(fr * (h @ w3.astype(f32) + b3.astype(f32)))
    return h, t


def hyena_filter_spectrum(hid, t, w_out_o):
    f32 = jnp.float32
    h = jnp.einsum('lf,fdw->ldw', hid, w_out_o.astype(f32))
    deltas = jnp.abs(jnp.linspace(HY_DECAY_MIN, HY_DECAY_MAX, HY_WIDTH, dtype=f32))
    h = h * jnp.exp(-t[:, None, None] * deltas)
    g = jnp.concatenate([h[:, 0], jnp.zeros((1, HY_WIDTH), f32), jnp.flip(h[1:, 1], axis=0)], axis=0)
    g = g / jnp.sum(jnp.abs(g), axis=0, keepdims=True)
    return jnp.fft.rfft(g, axis=0)


def long_conv(y, g_hat):
    n = y.shape[1]
    y_hat = jnp.fft.rfft(y, n=2 * n, axis=1)
    return jnp.fft.irfft(y_hat * g_hat[None], n=2 * n, axis=1)[:, :n]


def hyena_mixer(h, w_in, conv_w, conv_b, w1, b1, w2, b2, w3, b3, freq, filt_w_out, skip, w_out):
    f32 = jnp.float32
    n = h.shape[1]
    z = h @ w_in
    ch = z.shape[-1]
    z = lax.conv_general_dilated(z, conv_w[:, None, :].astype(z.dtype), window_strides=(1,),
                                 padding=[(HY_SHORT // 2, HY_SHORT // 2)],
                                 dimension_numbers=('NWC', 'WIO', 'NWC'), feature_group_count=ch) + conv_b
    x1, x2, v = jnp.split(z, 3, axis=-1)
    hid, t = hyena_filter_hidden(n, w1, b1, w2, b2, w3, b3, freq)
    y = v.astype(f32)
    for o, gate in enumerate((x1, x2)):
        g_hat = hyena_filter_spectrum(hid, t, filt_w_out[:, o])
        y = gate.astype(f32) * (long_conv(y, g_hat) + skip[o].astype(f32) * y)
    return y.astype(h.dtype) @ w_out


def setup_inputs(seed: int = 0) -> dict:
    keys = list(jax.random.split(jax.random.key(seed), 48))

    def nrm(shape, scale):
        return scale * jax.random.normal(keys.pop(), shape, jnp.float32)

    def gain(shape):
        return 1.0 + nrm(shape, 0.02)

    d, f = D_MODEL, D_FF
    g, p, gc = S5_GROUPS, S5_STATE, S5_GROUP
    fh = HY_FILT_HIDDEN
    return {
        "x": nrm((BATCH, SEQ, d), 1.0),
        "c": nrm((BATCH, d), 1.0),
        "ctx": nrm((BATCH, CTX_LEN, d), 1.0),
        "c_ctx": nrm((d,), 1.0),
        "ada_w": nrm((DEPTH, d, N_MOD * d), 0.5 * d ** -0.5),
        "ada_b": nrm((DEPTH, N_MOD * d), 0.02),
        "norm_g": gain((DEPTH, 3, d)),
        "ffn_w_in": nrm((DEPTH, 2, d, 2 * f), d ** -0.5),
        "ffn_w_out": nrm((DEPTH, 2, f, d), f ** -0.5),
        "ev_w_in": nrm((N_EVEN, d, EV_IN), d ** -0.5),
        "mla_q_a_norm_g": gain((N_EVEN, Q_LORA)),
        "mla_w_uq": nrm((N_EVEN, Q_LORA, MLA_HEADS * QK_DIM), Q_LORA ** -0.5),
        "mla_kv_a_norm_g": gain((N_EVEN, KV_LORA)),
        "mla_w_ukv": nrm((N_EVEN, KV_LORA, MLA_HEADS * (QK_NOPE + V_DIM)), KV_LORA ** -0.5),
        "mla_q_head_g": gain((N_EVEN, QK_DIM)),
        "mla_k_head_g": gain((N_EVEN, QK_DIM)),
        "s5_lam_re": -0.5 + nrm((N_EVEN, 2, g, p), 0.01),
        "s5_lam_im": jnp.pi * jnp.arange(p, dtype=jnp.float32) + nrm((N_EVEN, 2, g, p), 0.01),
        "s5_log_dt": jax.random.uniform(keys.pop(), (N_EVEN, 2, g), jnp.float32,
                                        math.log(DT_MIN), math.log(DT_MAX)),
        "s5_b_re": nrm((N_EVEN, 2, g, p, gc), (2 * gc) ** -0.5),
        "s5_b_im": nrm((N_EVEN, 2, g, p, gc), (2 * gc) ** -0.5),
        "s5_c_re": nrm((N_EVEN, 2, g, gc, p), p ** -0.5),
        "s5_c_im": nrm((N_EVEN, 2, g, gc, p), p ** -0.5),
        "s5_d": nrm((N_EVEN, S5_WIDTH), 1.0),
        "s5_glu_w": nrm((N_EVEN, S5_WIDTH, S5_WIDTH), S5_WIDTH ** -0.5),
        "s5_glu_b": nrm((N_EVEN, S5_WIDTH), 0.02),
        "ev_w_out": nrm((N_EVEN, EV_MIX, d), EV_MIX ** -0.5),
        "hy_w_in": nrm((N_ODD, d, 3 * HY_WIDTH), d ** -0.5),
        "hy_conv_w": nrm((N_ODD, HY_SHORT, 3 * HY_WIDTH), HY_SHORT ** -0.5),
        "hy_conv_b": nrm((N_ODD, 3 * HY_WIDTH), 0.02),
        "hy_filt_w1": nrm((N_ODD, HY_EMB, fh), HY_EMB ** -0.5),
        "hy_filt_b1": nrm((N_ODD, fh), 0.02),
        "hy_filt_w2": nrm((N_ODD, fh, fh), fh ** -0.5),
        "hy_filt_b2": nrm((N_ODD, fh), 0.02),
        "hy_filt_w3": nrm((N_ODD, fh, fh), fh ** -0.5),
        "hy_filt_b3": nrm((N_ODD, fh), 0.02),
        "hy_filt_freq": gain((N_ODD, fh)),
        "hy_filt_w_out": nrm((N_ODD, fh, HY_ORDER, 2, HY_WIDTH), fh ** -0.5),
        "hy_skip": nrm((N_ODD, HY_ORDER, HY_WIDTH), 1.0),
        "hy_w_out": nrm((N_ODD, HY_WIDTH, d), HY_WIDTH ** -0.5),
    }


def reference(x, c, ctx, c_ctx, ada_w, ada_b, norm_g, ffn_w_in, ffn_w_out,
              ev_w_in, mla_q_a_norm_g, mla_w_uq, mla_kv_a_norm_g, mla_w_ukv, mla_q_head_g, mla_k_head_g,
              s5_lam_re, s5_lam_im, s5_log_dt, s5_b_re, s5_b_im, s5_c_re, s5_c_im, s5_d, s5_glu_w, s5_glu_b,
              ev_w_out, hy_w_in, hy_conv_w, hy_conv_b, hy_filt_w1, hy_filt_b1, hy_filt_w2, hy_filt_b2,
              hy_filt_w3, hy_filt_b3, hy_filt_freq, hy_filt_w_out, hy_skip, hy_w_out):
    bsz = x.shape[0]
    ROWS = x.shape[1] // GRID_W
    rope = axial_rope_tables(ROWS, x.dtype)
    silu_c = jax.nn.silu(c)
    silu_cc = jax.nn.silu(c_ctx)
    cx = ctx
    for i in range(DEPTH):
        even = i % 2 == 0
        li = i // 2
        need_after = any(j % 2 == 0 for j in range(i + 1, DEPTH))
        use_ctx = even or need_after
        mx = (silu_c @ ada_w[i] + ada_b[i]).reshape(bsz, N_MOD, D_MODEL).transpose(1, 0, 2)[:, :, None, :]
        mc = (silu_cc @ ada_w[i] + ada_b[i]).reshape(N_MOD, 1, 1, D_MODEL)
        x = ffn_half(x, mx, 0, norm_g[i, 0], ffn_w_in[i, 0], ffn_w_out[i, 0])
        if use_ctx:
            cx = ffn_half(cx, mc, 0, norm_g[i, 0], ffn_w_in[i, 0], ffn_w_out[i, 0])
        hx = modulate(rms_norm(x, norm_g[i, 1]), mx[3], mx[4])
        hc = modulate(rms_norm(cx, norm_g[i, 1]), mc[3], mc[4]) if use_ctx else None
        if even:
            ox, oc = even_mixer(hx, hc, ev_w_in[li], mla_q_a_norm_g[li], mla_w_uq[li], mla_kv_a_norm_g[li],
                                mla_w_ukv[li], mla_q_head_g[li], mla_k_head_g[li],
                                s5_lam_re[li], s5_lam_im[li], s5_log_dt[li], s5_b_re[li], s5_b_im[li],
                                s5_c_re[li], s5_c_im[li], s5_d[li], s5_glu_w[li], s5_glu_b[li],
                                ev_w_out[li], rope, need_after)
        else:
            hy = (hy_w_in[li], hy_conv_w[li], hy_conv_b[li], hy_filt_w1[li], hy_filt_b1[li], hy_filt_w2[li],
                  hy_filt_b2[li], hy_filt_w3[li], hy_filt_b3[li], hy_filt_freq[li], hy_filt_w_out[li],
                  hy_skip[li], hy_w_out[li])
            ox = hyena_mixer(hx, *hy)
            oc = hyena_mixer(hc, *hy) if need_after else None
        x = x + mx[5] * ox
        if need_after:
            cx = cx + mc[5] * oc
            cx = ffn_half(cx, mc, 6, norm_g[i, 2], ffn_w_in[i, 1], ffn_w_out[i, 1])
        x = ffn_half(x, mx, 6, norm_g[i, 2], ffn_w_in[i, 1], ffn_w_out[i, 1])
    return x
```

```python
import functools
import math

import numpy as np
import jax
import jax.numpy as jnp
from jax import lax
from jax.experimental import pallas as pl
from jax.experimental.pallas import tpu as pltpu

F32 = jnp.float32
BF16 = jnp.bfloat16
HIGHEST = lax.Precision.HIGHEST

D_MODEL = 2048
N_MOD = 9
D_FF = 5632
NORM_EPS = 1e-6
GRID_W = 64
MLA_HEADS = 8
QK_NOPE = 128
QK_ROPE = 64
QK_DIM = QK_NOPE + QK_ROPE
V_DIM = 128
Q_LORA = 512
KV_LORA = 256
ROPE_BASE = 10000.0
S5_WIDTH = 1024
S5_GROUP = 16
S5_GROUPS = S5_WIDTH // S5_GROUP
S5_STATE = 64
S5_NSTATE = S5_GROUPS * S5_STATE
S5_CHUNKS = 4
HY_WIDTH = D_MODEL
HY_BANDS = 16
HY_FILT_HIDDEN = 64
HY_DECAY_MIN = math.log(1e-2) / 1.5
HY_DECAY_MAX = math.log(1e-2) / 0.3
DFT_N2 = 256
LANES = 128
SUBLANES = 8
VMEM_LIMIT = 56 * 1024 * 1024


def _cparams(*sem):
    return pltpu.CompilerParams(dimension_semantics=sem, vmem_limit_bytes=VMEM_LIMIT)


def _hdot(a, b):
    return jnp.dot(a, b, preferred_element_type=F32, precision=HIGHEST)


def _bdot(a, b):
    return jnp.dot(a, b, preferred_element_type=F32)


def _norm_mod(x, g, scale, shift):
    y = x * lax.rsqrt(jnp.mean(x * x, axis=-1, keepdims=True) + NORM_EPS) * g
    return y * (1.0 + scale) + shift


def _mods_kernel(s_ref, w_ref, b_ref, o_ref):
    s = s_ref[...]
    s = s * jax.nn.sigmoid(s)
    o_ref[...] = _hdot(s, w_ref[...]) + b_ref[...]


def _ada_mods(c, c_ctx, ada_w, ada_b):
    depth, d, nd = ada_w.shape
    s = jnp.zeros((SUBLANES, d), F32).at[0].set(c[0]).at[1].set(c_ctx)
    tn = 1024
    out = pl.pallas_call(
        _mods_kernel,
        grid=(depth, nd // tn),
        in_specs=[pl.BlockSpec((SUBLANES, d), lambda l, j: (0, 0)),
                  pl.BlockSpec((None, d, tn), lambda l, j: (l, 0, j)),
                  pl.BlockSpec((None, 1, tn), lambda l, j: (l, 0, j))],
        out_specs=pl.BlockSpec((None, SUBLANES, tn), lambda l, j: (l, 0, j)),
        out_shape=jax.ShapeDtypeStruct((depth, SUBLANES, nd), F32),
        compiler_params=_cparams("parallel", "parallel"),
        name="ada_mods",
    )(s, ada_w, ada_b.reshape(depth, 1, nd))
    return out[:, :2].reshape(depth, 2, N_MOD, 1, d)


def _ffn_kernel(x_ref, sh_ref, sc_ref, gt_ref, g_ref, wg_ref, wu_ref, wo_ref, o_ref, h_scr, acc_scr):
    j = pl.program_id(1)

    @pl.when(j == 0)
    def _():
        h_scr[...] = _norm_mod(x_ref[...], g_ref[...], sc_ref[...], sh_ref[...]).astype(BF16)
        acc_scr[...] = jnp.zeros_like(acc_scr)

    h = h_scr[...]
    g = _bdot(h, wg_ref[...])
    u = _bdot(h, wu_ref[...])
    a = (g * jax.nn.sigmoid(g)) * u
    acc_scr[...] += _bdot(a.astype(BF16), wo_ref[...])

    @pl.when(j == pl.num_programs(1) - 1)
    def _():
        o_ref[...] = x_ref[...] + gt_ref[...] * (0.5 * acc_scr[...])


def _ffn_half(x, mod, k0, g_norm, w_in, w_out):
    n, d = x.shape
    f = w_out.shape[0]
    tm = min(512, n)
    tf = 512
    nf = f // tf
    vec = pl.BlockSpec((1, d), lambda i, j: (0, 0))
    return pl.pallas_call(
        _ffn_kernel,
        grid=(n // tm, nf),
        in_specs=[pl.BlockSpec((tm, d), lambda i, j: (i, 0)), vec, vec, vec, vec,
                  pl.BlockSpec((d, tf), lambda i, j: (0, j)),
                  pl.BlockSpec((d, tf), lambda i, j: (0, j + nf)),
                  pl.BlockSpec((tf, d), lambda i, j: (j, 0))],
        out_specs=pl.BlockSpec((tm, d), lambda i, j: (i, 0)),
        out_shape=jax.ShapeDtypeStruct((n, d), F32),
        scratch_shapes=[pltpu.VMEM((tm, d), BF16), pltpu.VMEM((tm, d), F32)],
        compiler_params=_cparams("parallel", "arbitrary"),
        name="ffn_half",
    )(x, mod[k0], mod[k0 + 1], mod[k0 + 2], g_norm.reshape(1, d),
      w_in.astype(BF16), w_in.astype(BF16), w_out.astype(BF16))


def _proj_kernel(x_ref, sh_ref, sc_ref, g_ref, w_ref, o_ref, h_scr):
    @pl.when(pl.program_id(1) == 0)
    def _():
        h_scr[...] = _norm_mod(x_ref[...], g_ref[...], sc_ref[...], sh_ref[...]).astype(BF16)

    o_ref[...] = _bdot(h_scr[...], w_ref[...])


def _norm_proj(x, shift, scale, g_norm, w):
    n, d = x.shape
    nout = w.shape[1]
    tm = min(512, n)
    tn = 2048
    vec = pl.BlockSpec((1, d), lambda i, j: (0, 0))
    return pl.pallas_call(
        _proj_kernel,
        grid=(n // tm, nout // tn),
        in_specs=[pl.BlockSpec((tm, d), lambda i, j: (i, 0)), vec, vec, vec,
                  pl.BlockSpec((d, tn), lambda i, j: (0, j))],
        out_specs=pl.BlockSpec((tm, tn), lambda i, j: (i, j)),
        out_shape=jax.ShapeDtypeStruct((n, nout), F32),
        scratch_shapes=[pltpu.VMEM((tm, d), BF16)],
        compiler_params=_cparams("parallel", "arbitrary"),
        name="norm_proj",
    )(x, shift, scale, g_norm.reshape(1, d), w.astype(BF16))


def _mla_kernel(qa_ref, kva_ref, kpe_ref, cos_ref, sin_ref, gqa_ref, gkva_ref, wq_ref, wkv_ref,
                gq_ref, gk_ref, q_ref, k_ref, v_ref):
    def rms(t, g):
        return t * lax.rsqrt(jnp.mean(t * t, axis=-1, keepdims=True) + NORM_EPS) * g

    qn = rms(qa_ref[...], gqa_ref[...]).astype(BF16)
    kvn = rms(kva_ref[...], gkva_ref[...]).astype(BF16)
    cos = cos_ref[...]
    sin = sin_ref[...]
    gq = gq_ref[...]
    gk = gk_ref[...]
    kpe = kpe_ref[:, :LANES]
    kpe_sw = kpe_ref[:, LANES:]
    kpe_ss = jnp.sum(kpe * kpe, axis=-1, keepdims=True)
    k_rot = (kpe * gk[:, LANES:2 * LANES]) * cos + (kpe_sw * gk[:, 2 * LANES:]) * sin
    inv_dim = 1.0 / QK_DIM
    q_scale = 1.0 / math.sqrt(QK_DIM)
    for h in range(MLA_HEADS):
        qh = _bdot(qn, wq_ref[h])
        nope, pe, pe_sw = qh[:, :LANES], qh[:, LANES:2 * LANES], qh[:, 2 * LANES:]
        ss = jnp.sum(nope * nope, axis=-1, keepdims=True) + jnp.sum(pe * pe, axis=-1, keepdims=True)
        r = lax.rsqrt(ss * inv_dim + NORM_EPS) * q_scale
        q_rot = (pe * gq[:, LANES:2 * LANES]) * cos + (pe_sw * gq[:, 2 * LANES:]) * sin
        q_ref[h, :, :LANES] = (nope * r * gq[:, :LANES]).astype(BF16)
        q_ref[h, :, LANES:] = (q_rot * r).astype(BF16)
        kv = _bdot(kvn, wkv_ref[h])
        k_nope, vv = kv[:, :LANES], kv[:, LANES:]
        ssk = jnp.sum(k_nope * k_nope, axis=-1, keepdims=True) + kpe_ss
        rk = lax.rsqrt(ssk * inv_dim + NORM_EPS)
        k_ref[h, :, :LANES] = (k_nope * rk * gk[:, :LANES]).astype(BF16)
        k_ref[h, :, LANES:] = (k_rot * rk).astype(BF16)
        v_ref[h] = vv.astype(BF16)


_ROPE_SWAP = np.concatenate([np.arange(16, 32), np.arange(0, 16), np.arange(48, 64), np.arange(32, 48)])


def _pad_lanes(a, width=LANES):
    return jnp.pad(a, [(0, 0)] * (a.ndim - 1) + [(0, width - a.shape[-1])])


def _mla_weights(w_uq, w_ukv, q_head_g, k_head_g):
    wq = w_uq.reshape(Q_LORA, MLA_HEADS, QK_DIM).transpose(1, 0, 2)
    wq_pe = wq[:, :, QK_NOPE:]
    wq3 = jnp.concatenate([wq[:, :, :QK_NOPE], _pad_lanes(wq_pe), _pad_lanes(wq_pe[:, :, _ROPE_SWAP])], axis=-1)
    wkv = w_ukv.reshape(KV_LORA, MLA_HEADS, QK_NOPE + V_DIM).transpose(1, 0, 2)

    def g3(g):
        pe = g[QK_NOPE:]
        return jnp.concatenate([g[:QK_NOPE], _pad_lanes(pe), _pad_lanes(pe[_ROPE_SWAP])]).reshape(1, 3 * LANES)

    return wq3.astype(BF16), wkv.astype(BF16), g3(q_head_g), g3(k_head_g)


def _rope_tables(n):
    row = jnp.repeat(jnp.arange(n // GRID_W, dtype=F32), GRID_W)
    col = jnp.tile(jnp.arange(GRID_W, dtype=F32), n // GRID_W)
    n_freq = QK_ROPE // 4
    inv_freq = ROPE_BASE ** (-jnp.arange(n_freq, dtype=F32) / n_freq)
    ang_r = row[:, None] * inv_freq
    ang_c = col[:, None] * inv_freq
    cr, sr, cc, sc = jnp.cos(ang_r), jnp.sin(ang_r), jnp.cos(ang_c), jnp.sin(ang_c)
    cos = jnp.concatenate([cr, cr, cc, cc], axis=-1)
    sin = jnp.concatenate([-sr, sr, -sc, sc], axis=-1)
    return _pad_lanes(cos), _pad_lanes(sin)


def _mla_heads(proj, cos, sin, gqa, gkva, wq3, wkv, gq3, gk3):
    n = proj.shape[0]
    tm = min(512, n)
    hd = 2 * LANES
    full = lambda a: pl.BlockSpec(a.shape, lambda i: (0,) * a.ndim)
    return pl.pallas_call(
        _mla_kernel,
        grid=(n // tm,),
        in_specs=[pl.BlockSpec((tm, Q_LORA), lambda i: (i, S5_WIDTH // Q_LORA)),
                  pl.BlockSpec((tm, KV_LORA), lambda i: (i, (S5_WIDTH + Q_LORA) // KV_LORA)),
                  pl.BlockSpec((tm, 2 * LANES), lambda i: (i, (S5_WIDTH + Q_LORA + KV_LORA) // (2 * LANES))),
                  pl.BlockSpec((tm, LANES), lambda i: (i, 0)),
                  pl.BlockSpec((tm, LANES), lambda i: (i, 0)),
                  full(gqa), full(gkva), full(wq3), full(wkv), full(gq3), full(gk3)],
        out_specs=[pl.BlockSpec((MLA_HEADS, tm, hd), lambda i: (0, i, 0)),
                   pl.BlockSpec((MLA_HEADS, tm, hd), lambda i: (0, i, 0)),
                   pl.BlockSpec((MLA_HEADS, tm, V_DIM), lambda i: (0, i, 0))],
        out_shape=[jax.ShapeDtypeStruct((MLA_HEADS, n, hd), BF16),
                   jax.ShapeDtypeStruct((MLA_HEADS, n, hd), BF16),
                   jax.ShapeDtypeStruct((MLA_HEADS, n, V_DIM), BF16)],
        compiler_params=_cparams("parallel"),
        name="mla_heads",
    )(proj, proj, proj, cos, sin, gqa, gkva, wq3, wkv, gq3, gk3)


def _attn_kernel(q_ref, k_ref, v_ref, o_ref, m_scr, l_scr, acc_scr):
    kv = pl.program_id(2)

    @pl.when(kv == 0)
    def _():
        m_scr[...] = jnp.full_like(m_scr, -jnp.inf)
        l_scr[...] = jnp.zeros_like(l_scr)
        acc_scr[...] = jnp.zeros_like(acc_scr)

    s = lax.dot_general(q_ref[...], k_ref[...], (((1,), (1,)), ((), ())), preferred_element_type=F32)
    m_prev = m_scr[...]
    m_new = jnp.maximum(m_prev, jnp.max(s, axis=-1, keepdims=True))
    alpha = jnp.exp(m_prev - m_new)
    p = jnp.exp(s - m_new)
    l_scr[...] = alpha * l_scr[...] + jnp.sum(p, axis=-1, keepdims=True)
    acc_scr[...] = alpha * acc_scr[...] + _bdot(p.astype(BF16), v_ref[...])
    m_scr[...] = m_new

    @pl.when(kv == pl.num_programs(2) - 1)
    def _():
        o_ref[...] = acc_scr[...] / l_scr[...]


def _kv_tile(m):
    for t in (1280, 1024, 768, 512, 256, 128):
        if m % t == 0:
            return t
    raise ValueError(f"key length {m} is not a multiple of {LANES}")


def _attention(q, k, v):
    h, n, hd = q.shape
    m = k.shape[1]
    tq = min(512, n)
    tk = _kv_tile(m)
    return pl.pallas_call(
        _attn_kernel,
        grid=(h, n // tq, m // tk),
        in_specs=[pl.BlockSpec((None, tq, hd), lambda a, i, j: (a, i, 0)),
                  pl.BlockSpec((None, tk, hd), lambda a, i, j: (a, j, 0)),
                  pl.BlockSpec((None, tk, V_DIM), lambda a, i, j: (a, j, 0))],
        out_specs=pl.BlockSpec((tq, V_DIM), lambda a, i, j: (i, a)),
        out_shape=jax.ShapeDtypeStruct((n, h * V_DIM), F32),
        scratch_shapes=[pltpu.VMEM((tq, 1), F32), pltpu.VMEM((tq, 1), F32), pltpu.VMEM((tq, V_DIM), F32)],
        compiler_params=_cparams("parallel", "parallel", "arbitrary"),
        name="flash_attention",
    )(q, k, v)


S5_TILE = 256
S5_LANE_CHUNK = 512


def _s5_kernel(*refs, reverse, first):
    if first:
        (u_ref, d_ref, h0r_ref, h0i_ref, br_ref, bi_ref, cr_ref, ci_ref, tab_ref,
         y_ref, hr_ref, hi_ref, xr_scr, xi_scr) = refs
    else:
        (u_ref, yin_ref, h0r_ref, h0i_ref, br_ref, bi_ref, cr_ref, ci_ref, tab_ref,
         y_ref, hr_ref, hi_ref, xr_scr, xi_scr) = refs
    tm = u_ref.shape[0]
    nblk = tm // SUBLANES
    cin = S5_WIDTH // S5_CHUNKS
    cst = S5_NSTATE // S5_CHUNKS

    @pl.when(pl.program_id(0) == 0)
    def _():
        hr_ref[...] = h0r_ref[...]
        hi_ref[...] = h0i_ref[...]

    for q in range(S5_CHUNKS):
        uq = u_ref[:, q * cin:(q + 1) * cin].astype(BF16)
        xr_scr[:, q * cst:(q + 1) * cst] = _bdot(uq, br_ref[q])
        xi_scr[:, q * cst:(q + 1) * cst] = _bdot(uq, bi_ref[q])

    edge = 0 if reverse else SUBLANES - 1
    for jc in range(S5_NSTATE // S5_LANE_CHUNK):
        sl = slice(jc * S5_LANE_CHUNK, (jc + 1) * S5_LANE_CHUNK)

        def body(r, carry, sl=sl):
            car, cai = carry
            blk = (nblk - 1 - r) if reverse else r
            row = pl.multiple_of(blk * SUBLANES, SUBLANES)
            xr = xr_scr[pl.ds(row, SUBLANES), sl]
            xi = xi_scr[pl.ds(row, SUBLANES), sl]
            for idx, dist in enumerate((1, 2, 4)):
                ar = tab_ref[2 * idx, :, sl]
                ai = tab_ref[2 * idx + 1, :, sl]
                shift = SUBLANES - dist if reverse else dist
                sr = pltpu.roll(xr, shift, 0)
                si = pltpu.roll(xi, shift, 0)
                xr, xi = xr + ar * sr - ai * si, xi + ar * si + ai * sr
            pr = tab_ref[6, :, sl]
            pi = tab_ref[7, :, sl]
            xr, xi = xr + pr * car - pi * cai, xi + pr * cai + pi * car
            xr_scr[pl.ds(row, SUBLANES), sl] = xr
            xi_scr[pl.ds(row, SUBLANES), sl] = xi
            shape = (SUBLANES, S5_LANE_CHUNK)
            return (jnp.broadcast_to(xr[edge:edge + 1, :], shape), jnp.broadcast_to(xi[edge:edge + 1, :], shape))

        car, cai = lax.fori_loop(0, nblk, body, (hr_ref[:, sl], hi_ref[:, sl]), unroll=2)
        hr_ref[:, sl] = car
        hi_ref[:, sl] = cai

    cout = S5_WIDTH // S5_CHUNKS
    for q in range(S5_CHUNKS):
        hr = xr_scr[:, q * cst:(q + 1) * cst].astype(BF16)
        hi = xi_scr[:, q * cst:(q + 1) * cst].astype(BF16)
        y = _bdot(hr, cr_ref[q]) + _bdot(hi, ci_ref[q])
        cs = slice(q * cout, (q + 1) * cout)
        if first:
            base = u_ref[:, cs] * d_ref[:, cs]
        else:
            base = yin_ref[:, cs]
        y_ref[:, cs] = base + y


def _s5_scan(u_src, extra, h0, prm, *, reverse, first):
    n = u_src.shape[0]
    tm = min(S5_TILE, n)
    nt = n // tm
    tmap = (lambda i: (nt - 1 - i, 0)) if reverse else (lambda i: (i, 0))
    full = lambda a: pl.BlockSpec(a.shape, lambda i: (0,) * a.ndim)
    extra_spec = full(extra) if first else pl.BlockSpec((tm, S5_WIDTH), tmap)
    st = jax.ShapeDtypeStruct((SUBLANES, S5_NSTATE), F32)
    st_spec = pl.BlockSpec((SUBLANES, S5_NSTATE), lambda i: (0, 0))
    return pl.pallas_call(
        functools.partial(_s5_kernel, reverse=reverse, first=first),
        grid=(nt,),
        in_specs=[pl.BlockSpec((tm, S5_WIDTH), tmap), extra_spec, st_spec, st_spec,
                  full(prm["br"]), full(prm["bi"]), full(prm["cr"]), full(prm["ci"]), full(prm["tab"])],
        out_specs=[pl.BlockSpec((tm, S5_WIDTH), tmap), st_spec, st_spec],
        out_shape=[jax.ShapeDtypeStruct((n, S5_WIDTH), F32), st, st],
        scratch_shapes=[pltpu.VMEM((tm, S5_NSTATE), F32), pltpu.VMEM((tm, S5_NSTATE), F32)],
        compiler_params=_cparams("arbitrary"),
        name="s5_scan_rev" if reverse else "s5_scan_fwd",
    )(u_src, extra, h0[0], h0[1], prm["br"], prm["bi"], prm["cr"], prm["ci"], prm["tab"])


def _s5_params(lam_re, lam_im, log_dt, b_re, b_im, c_re, c_im, reverse):
    dt = jnp.exp(log_dt)[:, None]
    mag = jnp.exp(lam_re * dt)
    a_re = mag * jnp.cos(lam_im * dt)
    a_im = mag * jnp.sin(lam_im * dt)
    den = lam_re * lam_re + lam_im * lam_im
    nr = a_re - 1.0
    k_re = (nr * lam_re + a_im * lam_im) / den
    k_im = (a_im * lam_re - nr * lam_im) / den
    bx_re = k_re[:, :, None] * b_re - k_im[:, :, None] * b_im
    bx_im = k_re[:, :, None] * b_im + k_im[:, :, None] * b_re
    gpc = S5_GROUPS // S5_CHUNKS
    eye = jnp.eye(gpc, dtype=F32)

    def bdiag_in(b):
        b = b.reshape(S5_CHUNKS, gpc, S5_STATE, S5_GROUP)
        return jnp.einsum('qgpc,gh->qgchp', b, eye).reshape(S5_CHUNKS, gpc * S5_GROUP, gpc * S5_STATE)

    def bdiag_out(c):
        c = c.reshape(S5_CHUNKS, gpc, S5_GROUP, S5_STATE)
        return jnp.einsum('qgcp,gh->qgphc', c, eye).reshape(S5_CHUNKS, gpc * S5_STATE, gpc * S5_GROUP)

    pr, pi = [a_re.reshape(-1)], [a_im.reshape(-1)]
    for _ in range(SUBLANES - 1):
        pr, pi = pr + [pr[-1] * pr[0] - pi[-1] * pi[0]], pi + [pr[-1] * pi[0] + pi[-1] * pr[0]]
    t = np.arange(SUBLANES)
    tabs = []
    for dist in (1, 2, 4):
        keep = (t <= SUBLANES - 1 - dist) if reverse else (t >= dist)
        mask = jnp.asarray(keep, F32)[:, None]
        tabs += [mask * pr[dist - 1][None, :], mask * pi[dist - 1][None, :]]
    order = [SUBLANES - 1 - i for i in range(SUBLANES)] if reverse else list(range(SUBLANES))
    tabs += [jnp.stack([pr[i] for i in order]), jnp.stack([pi[i] for i in order])]
    return {"br": bdiag_in(bx_re).astype(BF16), "bi": bdiag_in(bx_im).astype(BF16),
            "cr": bdiag_out(c_re).astype(BF16), "ci": bdiag_out(-c_im).astype(BF16),
            "tab": jnp.stack(tabs)}


def _gelu(x):
    return x * (0.5 * (1.0 + jnp.tanh(math.sqrt(2.0 / math.pi) * (x + 0.044715 * (x * x * x)))))


def _evout_kernel(x_ref, att_ref, y_ref, gt_ref, gw_ref, gb_ref, wa_ref, ws_ref, o_ref):
    g = _gelu(y_ref[...])
    z = _bdot(g.astype(BF16), gw_ref[...]) + gb_ref[...]
    ssm = g * jax.nn.sigmoid(z)
    o = _bdot(att_ref[...].astype(BF16), wa_ref[...]) + _bdot(ssm.astype(BF16), ws_ref[...])
    o_ref[...] = x_ref[...] + gt_ref[...] * o


def _even_out(x, att, y, gate, glu_w, glu_b, w_out):
    n, d = x.shape
    tm = min(512, n)
    wa = w_out[:MLA_HEADS * V_DIM].astype(BF16)
    ws = w_out[MLA_HEADS * V_DIM:].astype(BF16)
    gw = glu_w.astype(BF16)
    gb = glu_b.reshape(1, S5_WIDTH)
    full = lambda a: pl.BlockSpec(a.shape, lambda i: (0,) * a.ndim)
    return pl.pallas_call(
        _evout_kernel,
        grid=(n // tm,),
        in_specs=[pl.BlockSpec((tm, d), lambda i: (i, 0)),
                  pl.BlockSpec((tm, MLA_HEADS * V_DIM), lambda i: (i, 0)),
                  pl.BlockSpec((tm, S5_WIDTH), lambda i: (i, 0)),
                  full(gate), full(gw), full(gb), full(wa), full(ws)],
        out_specs=pl.BlockSpec((tm, d), lambda i: (i, 0)),
        out_shape=jax.ShapeDtypeStruct((n, d), F32),
        compiler_params=_cparams("parallel"),
        name="even_out",
    )(x, att, y, gate, gw, gb, wa, ws)


def _resid_kernel(x_ref, y_ref, gt_ref, w_ref, o_ref):
    o_ref[...] = x_ref[...] + gt_ref[...] * _bdot(y_ref[...].astype(BF16), w_ref[...])


def _resid_proj(x, y, gate, w):
    n, d = x.shape
    tm = min(512, n)
    wb = w.astype(BF16)
    full = lambda a: pl.BlockSpec(a.shape, lambda i: (0,) * a.ndim)
    return pl.pallas_call(
        _resid_kernel,
        grid=(n // tm,),
        in_specs=[pl.BlockSpec((tm, d), lambda i: (i, 0)), pl.BlockSpec((tm, y.shape[1]), lambda i: (i, 0)),
                  full(gate), full(wb)],
        out_specs=pl.BlockSpec((tm, d), lambda i: (i, 0)),
        out_shape=jax.ShapeDtypeStruct((n, d), F32),
        compiler_params=_cparams("parallel"),
        name="resid_proj",
    )(x, y, gate, wb)


def _sconv_kernel(z_ref, zp_ref, zn_ref, w_ref, b_ref, o_ref):
    i = pl.program_id(0)
    z = z_ref[...]
    tm = z.shape[0]
    rows = lax.broadcasted_iota(jnp.int32, z.shape, 0)
    prev_row = jnp.where(i > 0, zp_ref[SUBLANES - 1:SUBLANES, :], 0.0)
    next_row = jnp.where(i < pl.num_programs(0) - 1, zn_ref[0:1, :], 0.0)
    z_dn = jnp.where(rows == 0, prev_row, pltpu.roll(z, 1, 0))
    z_up = jnp.where(rows == tm - 1, next_row, pltpu.roll(z, tm - 1, 0))
    o_ref[...] = w_ref[0:1, :] * z_dn + w_ref[1:2, :] * z + w_ref[2:3, :] * z_up + b_ref[...]


def _short_conv(z, conv_w, conv_b):
    n, ch = z.shape
    tm = min(512, n)
    tc = 2048
    per = tm // SUBLANES
    last = n // SUBLANES - 1
    return pl.pallas_call(
        _sconv_kernel,
        grid=(n // tm, ch // tc),
        in_specs=[pl.BlockSpec((tm, tc), lambda i, j: (i, j)),
                  pl.BlockSpec((SUBLANES, tc), lambda i, j: (jnp.maximum(i * per - 1, 0), j)),
                  pl.BlockSpec((SUBLANES, tc), lambda i, j: (jnp.minimum((i + 1) * per, last), j)),
                  pl.BlockSpec((conv_w.shape[0], tc), lambda i, j: (0, j)),
                  pl.BlockSpec((1, tc), lambda i, j: (0, j))],
        out_specs=pl.BlockSpec((tm, tc), lambda i, j: (i, j)),
        out_shape=jax.ShapeDtypeStruct((n, ch), F32),
        compiler_params=_cparams("parallel", "parallel"),
        name="short_conv",
    )(z, z, z, conv_w, conv_b.reshape(1, ch))


def _filt_kernel(bands_ref, w1t_ref, w1c_ref, w1s_ref, b1_ref, w2_ref, b2_ref, w3_ref, b3_ref, fr_ref,
                 wo_ref, dl_ref, g_ref, s_ref, *, seq):
    i = pl.program_id(0)
    tl = g_ref.shape[0]
    m = i * tl + lax.broadcasted_iota(jnp.int32, (tl, 1), 0)
    lag = jnp.where(m < seq, m, 2 * seq - m)
    t = lag.astype(F32) / seq
    ang = ((2.0 * math.pi) * t) * bands_ref[...]
    fr = fr_ref[...]
    pre = t * w1t_ref[...] + _hdot(jnp.cos(ang), w1c_ref[...]) + _hdot(-jnp.sin(ang), w1s_ref[...]) + b1_ref[...]
    h = jnp.sin(fr * pre)
    h = jnp.sin(fr * (_hdot(h, w2_ref[...]) + b2_ref[...]))
    h = jnp.sin(fr * (_hdot(h, w3_ref[...]) + b3_ref[...]))
    f = _hdot(h, wo_ref[...]) * jnp.exp(-t * dl_ref[...])
    f = jnp.where(m == seq, 0.0, f)
    g_ref[...] = f

    @pl.when(i == 0)
    def _():
        s_ref[...] = jnp.zeros_like(s_ref)

    s_ref[...] += jnp.sum(jnp.abs(f), axis=0, keepdims=True)


def _hyena_filter(seq, w1, b1, w2, b2, w3, b3, freq, w_out_o):
    fh = HY_FILT_HIDDEN
    w = HY_WIDTH
    tl = min(512, seq)
    pad2 = lambda a: jnp.pad(a, ((0, LANES - a.shape[0]), (0, LANES - a.shape[1])))
    padv = lambda a: _pad_lanes(a.reshape(1, -1))
    bands = padv(jnp.linspace(1e-4, HY_BANDS - 1, HY_BANDS, dtype=F32))
    w1t = padv(w1[0])
    w1c = pad2(w1[1:1 + HY_BANDS])
    w1s = pad2(w1[1 + HY_BANDS:])
    wo = jnp.pad(w_out_o.transpose(1, 0, 2), ((0, 0), (0, LANES - fh), (0, 0)))
    deltas = jnp.abs(jnp.linspace(HY_DECAY_MIN, HY_DECAY_MAX, w, dtype=F32)).reshape(1, w)
    nt = 2 * seq // tl
    half = seq // tl
    full = lambda a: pl.BlockSpec(a.shape, lambda i: (0,) * a.ndim)
    args = (bands, w1t, w1c, w1s, padv(b1), pad2(w2), padv(b2), pad2(w3), padv(b3), padv(freq))
    return pl.pallas_call(
        functools.partial(_filt_kernel, seq=seq),
        grid=(nt,),
        in_specs=[full(a) for a in args] + [pl.BlockSpec((None, LANES, w), lambda i: (i // half, 0, 0)),
                                            full(deltas)],
        out_specs=[pl.BlockSpec((tl, w), lambda i: (i, 0)), pl.BlockSpec((1, w), lambda i: (0, 0))],
        out_shape=[jax.ShapeDtypeStruct((2 * seq, w), F32), jax.ShapeDtypeStruct((1, w), F32)],
        compiler_params=_cparams("arbitrary"),
        name="hyena_filter",
    )(*args, wo, deltas)


def _dft_consts(seq):
    n = 2 * seq
    n2 = DFT_N2
    n1 = n // n2
    k1 = n1 // 2 + 1
    two_pi = 2.0 * np.pi
    kk, nn = np.meshgrid(np.arange(k1), np.arange(n1), indexing="ij")
    ang_a = two_pi * ((kk * nn) % n1) / n1
    fa = (np.cos(ang_a), -np.sin(ang_a))
    k2, m2 = np.meshgrid(np.arange(n2), np.arange(n2), indexing="ij")
    ang_c = two_pi * ((k2 * m2) % n2) / n2
    fc = (np.cos(ang_c), -np.sin(ang_c))
    kk, mm = np.meshgrid(np.arange(k1), np.arange(n2), indexing="ij")
    ang_t = two_pi * ((kk * mm) % n) / n
    tw = (np.cos(ang_t), -np.sin(ang_t))
    weight = np.where((np.arange(k1) == 0) | (np.arange(k1) == n1 // 2), 1.0, 2.0) / n
    rows, cols = np.meshgrid(np.arange(n1 // 2), np.arange(k1), indexing="ij")
    ang_f = two_pi * ((rows * cols) % n1) / n1
    mc = np.cos(ang_f) * weight[None, :]
    ms = -np.sin(ang_f) * weight[None, :]
    f32 = lambda a: jnp.asarray(a, F32)
    return {"n1": n1, "k1": k1,
            "fa_r": f32(fa[0]), "fa_i": f32(fa[1]), "fc_r": f32(fc[0]), "fc_i": f32(fc[1]),
            "twa_r": f32(tw[0].T[:, :, None]), "twa_i": f32(tw[1].T[:, :, None]),
            "twc_r": f32(tw[0][:, :, None]), "twc_i": f32(tw[1][:, :, None]),
            "mc": f32(mc), "ms": f32(ms)}


def _dfta_kernel(x_ref, fr_ref, fi_ref, twr_ref, twi_ref, or_ref, oi_ref):
    fr = fr_ref[...]
    fi = fi_ref[...]
    for s in range(SUBLANES):
        xs = x_ref[:, s, :]
        ar = _hdot(fr, xs)
        ai = _hdot(fi, xs)
        tr = twr_ref[s]
        ti = twi_ref[s]
        or_ref[:, s, :] = ar * tr - ai * ti
        oi_ref[:, s, :] = ar * ti + ai * tr


def _dft_stage_a(x3, col0, dc, n1c):
    k1 = dc["k1"]
    tw = 1024
    wt = HY_WIDTH // tw
    fr, fi = dc["fa_r"][:, :n1c], dc["fa_i"][:, :n1c]
    out = jax.ShapeDtypeStruct((k1, DFT_N2, HY_WIDTH), F32)
    return pl.pallas_call(
        _dfta_kernel,
        grid=(DFT_N2 // SUBLANES, wt),
        in_specs=[pl.BlockSpec((n1c, SUBLANES, tw), lambda b, j: (0, b, col0 * wt + j)),
                  pl.BlockSpec((k1, n1c), lambda b, j: (0, 0)),
                  pl.BlockSpec((k1, n1c), lambda b, j: (0, 0)),
                  pl.BlockSpec((SUBLANES, k1, 1), lambda b, j: (b, 0, 0)),
                  pl.BlockSpec((SUBLANES, k1, 1), lambda b, j: (b, 0, 0))],
        out_specs=[pl.BlockSpec((k1, SUBLANES, tw), lambda b, j: (0, b, j))] * 2,
        out_shape=[out, out],
        compiler_params=_cparams("parallel", "parallel"),
        name="dft_stage_a",
    )(x3, fr, fi, dc["twa_r"], dc["twa_i"])


def _dftc_filt_kernel(ar_ref, ai_ref, fr_ref, fi_ref, s_ref, gr_ref, gi_ref):
    ar = ar_ref[...]
    ai = ai_ref[...]
    fr = fr_ref[...]
    fi = fi_ref[...]
    inv = 1.0 / s_ref[...]
    gr_ref[...] = (_hdot(fr, ar) - _hdot(fi, ai)) * inv
    gi_ref[...] = (_hdot(fr, ai) + _hdot(fi, ar)) * inv


def _dft_filter_spectrum(ar, ai, abs_sum, dc):
    k1 = dc["k1"]
    tw = 1024
    blk = pl.BlockSpec((None, DFT_N2, tw), lambda k, j: (k, 0, j))
    mat = pl.BlockSpec((DFT_N2, DFT_N2), lambda k, j: (0, 0))
    out = jax.ShapeDtypeStruct((k1, DFT_N2, HY_WIDTH), F32)
    return pl.pallas_call(
        _dftc_filt_kernel,
        grid=(k1, HY_WIDTH // tw),
        in_specs=[blk, blk, mat, mat, pl.BlockSpec((1, tw), lambda k, j: (0, j))],
        out_specs=[blk, blk],
        out_shape=[out, out],
        compiler_params=_cparams("parallel", "parallel"),
        name="dft_filter_spectrum",
    )(ar, ai, dc["fc_r"], dc["fc_i"], abs_sum)


def _dftmid_kernel(ar_ref, ai_ref, gr_ref, gi_ref, fr_ref, fi_ref, twr_ref, twi_ref, or_ref, oi_ref):
    ar = ar_ref[...]
    ai = ai_ref[...]
    fr = fr_ref[...]
    fi = fi_ref[...]
    xr = _hdot(fr, ar) - _hdot(fi, ai)
    xi = _hdot(fr, ai) + _hdot(fi, ar)
    gr = gr_ref[...]
    gi = gi_ref[...]
    yr = xr * gr - xi * gi
    yi = xr * gi + xi * gr
    br = _hdot(fr, yr) + _hdot(fi, yi)
    bi = _hdot(fr, yi) - _hdot(fi, yr)
    tr = twr_ref[...]
    ti = twi_ref[...]
    or_ref[...] = br * tr + bi * ti
    oi_ref[...] = bi * tr - br * ti


def _dft_middle(ar, ai, gr, gi, dc):
    k1 = dc["k1"]
    tw = 1024
    blk = pl.BlockSpec((None, DFT_N2, tw), lambda k, j: (k, 0, j))
    mat = pl.BlockSpec((DFT_N2, DFT_N2), lambda k, j: (0, 0))
    twd = pl.BlockSpec((None, DFT_N2, 1), lambda k, j: (k, 0, 0))
    out = jax.ShapeDtypeStruct((k1, DFT_N2, HY_WIDTH), F32)
    return pl.pallas_call(
        _dftmid_kernel,
        grid=(k1, HY_WIDTH // tw),
        in_specs=[blk, blk, blk, blk, mat, mat, twd, twd],
        out_specs=[blk, blk],
        out_shape=[out, out],
        compiler_params=_cparams("parallel", "parallel"),
        name="dft_middle",
    )(ar, ai, gr, gi, dc["fc_r"], dc["fc_i"], dc["twc_r"], dc["twc_i"])


def _dftf_kernel(br_ref, bi_ref, mc_ref, ms_ref, gate_ref, y_ref, skip_ref, o_ref):
    mc = mc_ref[...]
    ms = ms_ref[...]
    skip = skip_ref[...]
    for s in range(SUBLANES):
        conv = _hdot(mc, br_ref[:, s, :]) + _hdot(ms, bi_ref[:, s, :])
        o_ref[:, s, :] = gate_ref[:, s, :] * (conv + skip * y_ref[:, s, :])


def _dft_final(br, bi, dc, gate3, gate_col, y3, y_col, skip):
    k1 = dc["k1"]
    nh = dc["n1"] // 2
    tw = 1024
    wt = HY_WIDTH // tw
    bblk = pl.BlockSpec((k1, SUBLANES, tw), lambda b, j: (0, b, j))
    mat = pl.BlockSpec((nh, k1), lambda b, j: (0, 0))
    return pl.pallas_call(
        _dftf_kernel,
        grid=(DFT_N2 // SUBLANES, wt),
        in_specs=[bblk, bblk, mat, mat,
                  pl.BlockSpec((nh, SUBLANES, tw), lambda b, j: (0, b, gate_col * wt + j)),
                  pl.BlockSpec((nh, SUBLANES, tw), lambda b, j: (0, b, y_col * wt + j)),
                  pl.BlockSpec((1, tw), lambda b, j: (0, j))],
        out_specs=pl.BlockSpec((nh, SUBLANES, tw), lambda b, j: (0, b, j)),
        out_shape=jax.ShapeDtypeStruct((nh, DFT_N2, HY_WIDTH), F32),
        compiler_params=_cparams("parallel", "parallel"),
        name="dft_final",
    )(br, bi, dc["mc"], dc["ms"], gate3, y3, skip)


def _hyena_mixer(x, shift, scale, gate, g_norm, w_in, conv_w, conv_b, w1, b1, w2, b2, w3, b3, freq,
                 filt_w_out, skip, w_out):
    seq = x.shape[0]
    dc = _dft_consts(seq)
    nh = dc["n1"] // 2
    z = _norm_proj(x, shift, scale, g_norm, w_in)
    zc = _short_conv(z, conv_w, conv_b)
    zc3 = zc.reshape(nh, DFT_N2, 3 * HY_WIDTH)
    y3, y_col = zc3, 2
    for o in range(2):
        g, abs_sum = _hyena_filter(seq, w1, b1, w2, b2, w3, b3, freq, filt_w_out[:, o])
        ga_r, ga_i = _dft_stage_a(g.reshape(dc["n1"], DFT_N2, HY_WIDTH), 0, dc, dc["n1"])
        gr, gi = _dft_filter_spectrum(ga_r, ga_i, abs_sum, dc)
        ya_r, ya_i = _dft_stage_a(y3, y_col, dc, nh)
        br, bi = _dft_middle(ya_r, ya_i, gr, gi, dc)
        y3 = _dft_final(br, bi, dc, zc3, o, y3, y_col, skip[o].reshape(1, HY_WIDTH))
        y_col = 0
    return _resid_proj(x, y3.reshape(seq, HY_WIDTH), gate, w_out)


def _even_mixer(x, cx, mx, mc, g_norm, ev_w_in, q_a_norm_g, w_uq, kv_a_norm_g, w_ukv, q_head_g, k_head_g,
                lam_re, lam_im, log_dt, b_re, b_im, c_re, c_im, d_skip, glu_w, glu_b, ev_w_out):
    n = x.shape[0]
    nc = cx.shape[0]
    cuts = (Q_LORA, Q_LORA + KV_LORA, Q_LORA + KV_LORA + QK_ROPE)
    w_qa, w_kva, w_kpe, w_u = (ev_w_in[:, :cuts[0]], ev_w_in[:, cuts[0]:cuts[1]],
                               ev_w_in[:, cuts[1]:cuts[2]], ev_w_in[:, cuts[2]:])
    w_all = jnp.concatenate([w_u, w_qa, w_kva, _pad_lanes(w_kpe), _pad_lanes(w_kpe[:, _ROPE_SWAP])], axis=1)
    proj_x = _norm_proj(x, mx[3], mx[4], g_norm, w_all)
    proj_c = _norm_proj(cx, mc[3], mc[4], g_norm, w_all)

    wq3, wkv, gq3, gk3 = _mla_weights(w_uq, w_ukv, q_head_g, k_head_g)
    gqa = q_a_norm_g.reshape(1, Q_LORA)
    gkva = kv_a_norm_g.reshape(1, KV_LORA)
    cos_x, sin_x = _rope_tables(n)
    cos_c = _pad_lanes(jnp.ones((nc, QK_ROPE), F32))
    sin_c = jnp.zeros((nc, LANES), F32)
    q_x, k_x, v_x = _mla_heads(proj_x, cos_x, sin_x, gqa, gkva, wq3, wkv, gq3, gk3)
    _, k_c, v_c = _mla_heads(proj_c, cos_c, sin_c, gqa, gkva, wq3, wkv, gq3, gk3)
    att = _attention(q_x, jnp.concatenate([k_x, k_c], axis=1), jnp.concatenate([v_x, v_c], axis=1))

    d_row = d_skip.reshape(1, S5_WIDTH)
    zero = jnp.zeros((SUBLANES, S5_NSTATE), F32)
    y = None
    for direction, reverse in ((0, False), (1, True)):
        prm = _s5_params(lam_re[direction], lam_im[direction], log_dt[direction], b_re[direction],
                         b_im[direction], c_re[direction], c_im[direction], reverse)
        _, hc_r, hc_i = _s5_scan(proj_c, d_row, (zero, zero), prm, reverse=reverse, first=True)
        if y is None:
            y, _, _ = _s5_scan(proj_x, d_row, (hc_r, hc_i), prm, reverse=reverse, first=True)
        else:
            y, _, _ = _s5_scan(proj_x, y, (hc_r, hc_i), prm, reverse=reverse, first=False)
    return _even_out(x, att, y, mx[5], glu_w, glu_b, ev_w_out)


def kernel(x, c, ctx, c_ctx, ada_w, ada_b, norm_g, ffn_w_in, ffn_w_out, ev_w_in, mla_q_a_norm_g, mla_w_uq, mla_kv_a_norm_g, mla_w_ukv, mla_q_head_g, mla_k_head_g, s5_lam_re, s5_lam_im, s5_log_dt, s5_b_re, s5_b_im, s5_c_re, s5_c_im, s5_d, s5_glu_w, s5_glu_b, ev_w_out, hy_w_in, hy_conv_w, hy_conv_b, hy_filt_w1, hy_filt_b1, hy_filt_w2, hy_filt_b2, hy_filt_w3, hy_filt_b3, hy_filt_freq, hy_filt_w_out, hy_skip, hy_w_out):
    assert x.shape[0] == 1 and ctx.shape[0] == 1, "kernels are written for batch 1"
    depth = ada_w.shape[0]
    mods = _ada_mods(c, c_ctx, ada_w, ada_b)
    xs = x[0]
    cx = ctx[0]
    for i in range(depth):
        even = i % 2 == 0
        li = i // 2
        need_after = any(j % 2 == 0 for j in range(i + 1, depth))
        use_ctx = even or need_after
        assert not need_after, "context outputs after an even layer are not needed at this depth"
        mx = mods[i, 0]
        mc = mods[i, 1]
        xs = _ffn_half(xs, mx, 0, norm_g[i, 0], ffn_w_in[i, 0], ffn_w_out[i, 0])
        if use_ctx:
            cx = _ffn_half(cx, mc, 0, norm_g[i, 0], ffn_w_in[i, 0], ffn_w_out[i, 0])
        if even:
            xs = _even_mixer(xs, cx, mx, mc, norm_g[i, 1], ev_w_in[li], mla_q_a_norm_g[li], mla_w_uq[li],
                             mla_kv_a_norm_g[li], mla_w_ukv[li], mla_q_head_g[li], mla_k_head_g[li],
                             s5_lam_re[li], s5_lam_im[li], s5_log_dt[li], s5_b_re[li], s5_b_im[li],
                             s5_c_re[li], s5_c_im[li], s5_d[li], s5_glu_w[li], s5_glu_b[li], ev_w_out[li])
        else:
            xs = _hyena_mixer(xs, mx[3], mx[4], mx[5], norm_g[i, 1], hy_w_in[li], hy_conv_w[li], hy_conv_b[li],
                              hy_filt_w1[li], hy_filt_b1[li], hy_filt_w2[li], hy_filt_b2[li], hy_filt_w3[li],
                              hy_filt_b3[li], hy_filt_freq[li], hy_filt_w_out[li], hy_skip[li], hy_w_out[li])
        xs = _ffn_half(xs, mx, 6, norm_g[i, 2], ffn_w_in[i, 1], ffn_w_out[i, 1])
    return xs[None]
```

```python
import functools
import math

import numpy as np
import jax
import jax.numpy as jnp
from jax import lax
from jax.experimental import pallas as pl
from jax.experimental.pallas import tpu as pltpu

F32 = jnp.float32
BF16 = jnp.bfloat16
HIGHEST = lax.Precision.HIGHEST

D_MODEL = 2048
N_MOD = 9
D_FF = 5632
NORM_EPS = 1e-6
GRID_W = 64
MLA_HEADS = 8
QK_NOPE = 128
QK_ROPE = 64
QK_DIM = QK_NOPE + QK_ROPE
V_DIM = 128
Q_LORA = 512
KV_LORA = 256
ROPE_BASE = 10000.0
S5_WIDTH = 1024
S5_GROUP = 16
S5_GROUPS = S5_WIDTH // S5_GROUP
S5_STATE = 64
S5_NSTATE = S5_GROUPS * S5_STATE
S5_CHUNKS = 4
HY_WIDTH = D_MODEL
HY_BANDS = 16
HY_FILT_HIDDEN = 64
HY_DECAY_MIN = math.log(1e-2) / 1.5
HY_DECAY_MAX = math.log(1e-2) / 0.3
DFT_N2 = 256
LANES = 128
SUBLANES = 8
VMEM_LIMIT = 56 * 1024 * 1024


def _cparams(*sem):
    return pltpu.CompilerParams(dimension_semantics=sem, vmem_limit_bytes=VMEM_LIMIT)


def _hdot(a, b):
    return jnp.dot(a, b, preferred_element_type=F32, precision=HIGHEST)


def _bdot(a, b):
    return jnp.dot(a, b, preferred_element_type=F32)


def _split(x):
    hi = x.astype(BF16)
    return hi, (x - hi.astype(F32)).astype(BF16)


def _dot3(a, b):
    return _bdot(a[0], b[0]) + (_bdot(a[1], b[0]) + _bdot(a[0], b[1]))


def _norm_mod(x, g, scale, shift):
    y = x * lax.rsqrt(jnp.mean(x * x, axis=-1, keepdims=True) + NORM_EPS) * g
    return y * (1.0 + scale) + shift


def _mods_kernel(s_ref, w_ref, b_ref, o_ref):
    s = s_ref[...]
    s = s * jax.nn.sigmoid(s)
    o_ref[...] = _hdot(s, w_ref[...]) + b_ref[...]


def _ada_mods(c, c_ctx, ada_w, ada_b):
    depth, d, nd = ada_w.shape
    s = jnp.zeros((SUBLANES, d), F32).at[0].set(c[0]).at[1].set(c_ctx)
    tn = 1024
    out = pl.pallas_call(
        _mods_kernel,
        grid=(depth, nd // tn),
        in_specs=[pl.BlockSpec((SUBLANES, d), lambda l, j: (0, 0)),
                  pl.BlockSpec((None, d, tn), lambda l, j: (l, 0, j)),
                  pl.BlockSpec((None, 1, tn), lambda l, j: (l, 0, j))],
        out_specs=pl.BlockSpec((None, SUBLANES, tn), lambda l, j: (l, 0, j)),
        out_shape=jax.ShapeDtypeStruct((depth, SUBLANES, nd), F32),
        compiler_params=_cparams("parallel", "parallel"),
        name="ada_mods",
    )(s, ada_w, ada_b.reshape(depth, 1, nd))
    return out[:, :2].reshape(depth, 2, N_MOD, 1, d)


def _ffn_kernel(x_ref, sh_ref, sc_ref, gt_ref, g_ref, wg_ref, wu_ref, wo_ref, o_ref, h_scr, acc_scr):
    j = pl.program_id(1)

    @pl.when(j == 0)
    def _():
        h_scr[...] = _norm_mod(x_ref[...], g_ref[...], sc_ref[...], sh_ref[...]).astype(BF16)
        acc_scr[...] = jnp.zeros_like(acc_scr)

    h = h_scr[...]
    g = _bdot(h, wg_ref[...])
    u = _bdot(h, wu_ref[...])
    a = (g * jax.nn.sigmoid(g)) * u
    acc_scr[...] += _bdot(a.astype(BF16), wo_ref[...])

    @pl.when(j == pl.num_programs(1) - 1)
    def _():
        o_ref[...] = x_ref[...] + gt_ref[...] * (0.5 * acc_scr[...])


def _ffn_half(x, mod, k0, g_norm, w_in, w_out):
    n, d = x.shape
    f = w_out.shape[0]
    tm = min(512, n)
    tf = 512
    nf = f // tf
    vec = pl.BlockSpec((1, d), lambda i, j: (0, 0))
    return pl.pallas_call(
        _ffn_kernel,
        grid=(n // tm, nf),
        in_specs=[pl.BlockSpec((tm, d), lambda i, j: (i, 0)), vec, vec, vec, vec,
                  pl.BlockSpec((d, tf), lambda i, j: (0, j)),
                  pl.BlockSpec((d, tf), lambda i, j: (0, j + nf)),
                  pl.BlockSpec((tf, d), lambda i, j: (j, 0))],
        out_specs=pl.BlockSpec((tm, d), lambda i, j: (i, 0)),
        out_shape=jax.ShapeDtypeStruct((n, d), F32),
        scratch_shapes=[pltpu.VMEM((tm, d), BF16), pltpu.VMEM((tm, d), F32)],
        compiler_params=_cparams("parallel", "arbitrary"),
        name="ffn_half",
    )(x, mod[k0], mod[k0 + 1], mod[k0 + 2], g_norm.reshape(1, d),
      w_in.astype(BF16), w_in.astype(BF16), w_out.astype(BF16))


def _proj_kernel(x_ref, sh_ref, sc_ref, g_ref, w_ref, o_ref, h_scr):
    @pl.when(pl.program_id(1) == 0)
    def _():
        h_scr[...] = _norm_mod(x_ref[...], g_ref[...], sc_ref[...], sh_ref[...]).astype(BF16)

    o_ref[...] = _bdot(h_scr[...], w_ref[...])


def _norm_proj(x, shift, scale, g_norm, w):
    n, d = x.shape
    nout = w.shape[1]
    tm = min(512, n)
    tn = 2048
    vec = pl.BlockSpec((1, d), lambda i, j: (0, 0))
    return pl.pallas_call(
        _proj_kernel,
        grid=(n // tm, nout // tn),
        in_specs=[pl.BlockSpec((tm, d), lambda i, j: (i, 0)), vec, vec, vec,
                  pl.BlockSpec((d, tn), lambda i, j: (0, j))],
        out_specs=pl.BlockSpec((tm, tn), lambda i, j: (i, j)),
        out_shape=jax.ShapeDtypeStruct((n, nout), F32),
        scratch_shapes=[pltpu.VMEM((tm, d), BF16)],
        compiler_params=_cparams("parallel", "arbitrary"),
        name="norm_proj",
    )(x, shift, scale, g_norm.reshape(1, d), w.astype(BF16))


def _mla_kernel(qa_ref, kva_ref, kpe_ref, cos_ref, sin_ref, gqa_ref, gkva_ref, wq_ref, wkv_ref,
                gq_ref, gk_ref, q_ref, k_ref, v_ref):
    def rms(t, g):
        return t * lax.rsqrt(jnp.mean(t * t, axis=-1, keepdims=True) + NORM_EPS) * g

    qn = rms(qa_ref[...], gqa_ref[...]).astype(BF16)
    kvn = rms(kva_ref[...], gkva_ref[...]).astype(BF16)
    cos = cos_ref[...]
    sin = sin_ref[...]
    gq = gq_ref[...]
    gk = gk_ref[...]
    kpe = kpe_ref[:, :LANES]
    kpe_sw = kpe_ref[:, LANES:]
    kpe_ss = jnp.sum(kpe * kpe, axis=-1, keepdims=True)
    k_rot = (kpe * gk[:, LANES:2 * LANES]) * cos + (kpe_sw * gk[:, 2 * LANES:]) * sin
    inv_dim = 1.0 / QK_DIM
    q_scale = math.log2(math.e) / math.sqrt(QK_DIM)
    for h in range(MLA_HEADS):
        qh = _bdot(qn, wq_ref[h])
        nope, pe, pe_sw = qh[:, :LANES], qh[:, LANES:2 * LANES], qh[:, 2 * LANES:]
        ss = jnp.sum(nope * nope, axis=-1, keepdims=True) + jnp.sum(pe * pe, axis=-1, keepdims=True)
        r = lax.rsqrt(ss * inv_dim + NORM_EPS) * q_scale
        q_rot = (pe * gq[:, LANES:2 * LANES]) * cos + (pe_sw * gq[:, 2 * LANES:]) * sin
        q_ref[h, :, :LANES] = (nope * r * gq[:, :LANES]).astype(BF16)
        q_ref[h, :, LANES:] = (q_rot * r).astype(BF16)
        kv = _bdot(kvn, wkv_ref[h])
        k_nope, vv = kv[:, :LANES], kv[:, LANES:]
        ssk = jnp.sum(k_nope * k_nope, axis=-1, keepdims=True) + kpe_ss
        rk = lax.rsqrt(ssk * inv_dim + NORM_EPS)
        k_ref[h, :, :LANES] = (k_nope * rk * gk[:, :LANES]).astype(BF16)
        k_ref[h, :, LANES:] = (k_rot * rk).astype(BF16)
        v_ref[h] = vv.astype(BF16)


_ROPE_SWAP = np.concatenate([np.arange(16, 32), np.arange(0, 16), np.arange(48, 64), np.arange(32, 48)])


def _pad_lanes(a, width=LANES):
    return jnp.pad(a, [(0, 0)] * (a.ndim - 1) + [(0, width - a.shape[-1])])


def _mla_weights(w_uq, w_ukv, q_head_g, k_head_g):
    wq = w_uq.reshape(Q_LORA, MLA_HEADS, QK_DIM).transpose(1, 0, 2)
    wq_pe = wq[:, :, QK_NOPE:]
    wq3 = jnp.concatenate([wq[:, :, :QK_NOPE], _pad_lanes(wq_pe), _pad_lanes(wq_pe[:, :, _ROPE_SWAP])], axis=-1)
    wkv = w_ukv.reshape(KV_LORA, MLA_HEADS, QK_NOPE + V_DIM).transpose(1, 0, 2)

    def g3(g):
        pe = g[QK_NOPE:]
        return jnp.concatenate([g[:QK_NOPE], _pad_lanes(pe), _pad_lanes(pe[_ROPE_SWAP])]).reshape(1, 3 * LANES)

    return wq3.astype(BF16), wkv.astype(BF16), g3(q_head_g), g3(k_head_g)


def _rope_tables(n):
    row = jnp.repeat(jnp.arange(n // GRID_W, dtype=F32), GRID_W)
    col = jnp.tile(jnp.arange(GRID_W, dtype=F32), n // GRID_W)
    n_freq = QK_ROPE // 4
    inv_freq = ROPE_BASE ** (-jnp.arange(n_freq, dtype=F32) / n_freq)
    ang_r = row[:, None] * inv_freq
    ang_c = col[:, None] * inv_freq
    cr, sr, cc, sc = jnp.cos(ang_r), jnp.sin(ang_r), jnp.cos(ang_c), jnp.sin(ang_c)
    cos = jnp.concatenate([cr, cr, cc, cc], axis=-1)
    sin = jnp.concatenate([-sr, sr, -sc, sc], axis=-1)
    return _pad_lanes(cos), _pad_lanes(sin)


def _mla_heads(proj, cos, sin, gqa, gkva, wq3, wkv, gq3, gk3):
    n = proj.shape[0]
    tm = min(512, n)
    hd = 2 * LANES
    full = lambda a: pl.BlockSpec(a.shape, lambda i: (0,) * a.ndim)
    return pl.pallas_call(
        _mla_kernel,
        grid=(n // tm,),
        in_specs=[pl.BlockSpec((tm, Q_LORA), lambda i: (i, S5_WIDTH // Q_LORA)),
                  pl.BlockSpec((tm, KV_LORA), lambda i: (i, (S5_WIDTH + Q_LORA) // KV_LORA)),
                  pl.BlockSpec((tm, 2 * LANES), lambda i: (i, (S5_WIDTH + Q_LORA + KV_LORA) // (2 * LANES))),
                  pl.BlockSpec((tm, LANES), lambda i: (i, 0)),
                  pl.BlockSpec((tm, LANES), lambda i: (i, 0)),
                  full(gqa), full(gkva), full(wq3), full(wkv), full(gq3), full(gk3)],
        out_specs=[pl.BlockSpec((MLA_HEADS, tm, hd), lambda i: (0, i, 0)),
                   pl.BlockSpec((MLA_HEADS, tm, hd), lambda i: (0, i, 0)),
                   pl.BlockSpec((MLA_HEADS, tm, V_DIM), lambda i: (0, i, 0))],
        out_shape=[jax.ShapeDtypeStruct((MLA_HEADS, n, hd), BF16),
                   jax.ShapeDtypeStruct((MLA_HEADS, n, hd), BF16),
                   jax.ShapeDtypeStruct((MLA_HEADS, n, V_DIM), BF16)],
        compiler_params=_cparams("parallel"),
        name="mla_heads",
    )(proj, proj, proj, cos, sin, gqa, gkva, wq3, wkv, gq3, gk3)


def _attn_kernel(q_ref, k_ref, v_ref, o_ref, s_scr, bm_scr, m_scr, acc_scr, *, nkv):
    t = pl.program_id(0)
    tk = k_ref.shape[0]
    jp = lax.rem(t + (nkv - 1), nkv)

    @pl.when(t == 0)
    def _():
        s_scr[1] = jnp.zeros(s_scr.shape[1:], F32)
        bm_scr[1] = jnp.zeros(bm_scr.shape[1:], F32)
        m_scr[...] = jnp.zeros_like(m_scr)
        acc_scr[...] = jnp.zeros_like(acc_scr)

    def step(cur, prev):
        s_new = lax.dot_general(q_ref[...], k_ref[...], (((1,), (1,)), ((), ())), preferred_element_type=F32)
        s_scr[cur] = s_new
        bm_scr[cur] = jnp.max(s_new, axis=-1, keepdims=True)
        m_prev = jnp.where(jp == 0, -jnp.inf, m_scr[...])
        m_new = jnp.maximum(m_prev, bm_scr[prev])
        alpha = jnp.exp2(m_prev - m_new)
        p = jnp.exp2(s_scr[prev] - m_new).astype(BF16)
        v_ext = jnp.concatenate([v_ref[...], jnp.ones((tk, LANES), BF16)], axis=1)
        acc_scr[...] = alpha * acc_scr[...] + _bdot(p, v_ext)
        m_scr[...] = m_new

    @pl.when(lax.rem(t, 2) == 0)
    def _():
        step(0, 1)

    @pl.when(lax.rem(t, 2) == 1)
    def _():
        step(1, 0)

    @pl.when((jp == nkv - 1) & (t > 0))
    def _():
        acc = acc_scr[...]
        o_ref[...] = acc[:, :V_DIM] / acc[:, V_DIM:]


def _kv_tile(m):
    for t in (1280, 1024, 768, 512, 256, 128):
        if m % t == 0:
            return t
    raise ValueError(f"key length {m} is not a multiple of {LANES}")


def _attention(q, k, v):
    h, n, hd = q.shape
    m = k.shape[1]
    tq = min(1024, n)
    tk = _kv_tile(m)
    nq = n // tq
    nkv = m // tk
    total = h * nq * nkv

    def qk_idx(t):
        tt = jnp.minimum(t, total - 1)
        return tt // (nq * nkv), (tt // nkv) % nq, tt % nkv

    def pv_idx(t):
        tt = jnp.maximum(t - 1, 0)
        return tt // (nq * nkv), (tt // nkv) % nq, tt % nkv

    return pl.pallas_call(
        functools.partial(_attn_kernel, nkv=nkv),
        grid=(total + 1,),
        in_specs=[pl.BlockSpec((None, tq, hd), lambda t: (qk_idx(t)[0], qk_idx(t)[1], 0)),
                  pl.BlockSpec((None, tk, hd), lambda t: (qk_idx(t)[0], qk_idx(t)[2], 0)),
                  pl.BlockSpec((None, tk, V_DIM), lambda t: (pv_idx(t)[0], pv_idx(t)[2], 0))],
        out_specs=pl.BlockSpec((tq, V_DIM), lambda t: (pv_idx(t)[1], pv_idx(t)[0])),
        out_shape=jax.ShapeDtypeStruct((n, h * V_DIM), F32),
        scratch_shapes=[pltpu.VMEM((2, tq, tk), F32), pltpu.VMEM((2, tq, 1), F32),
                        pltpu.VMEM((tq, 1), F32), pltpu.VMEM((tq, V_DIM + LANES), F32)],
        compiler_params=_cparams("arbitrary"),
        name="flash_attention",
    )(q, k, v)


S5_TILE = 256
S5_LANE_CHUNK = 512


def _s5_kernel(*refs, reverse, first):
    if first:
        (u_ref, d_ref, h0r_ref, h0i_ref, br_ref, bi_ref, cr_ref, ci_ref, tab_ref,
         y_ref, hr_ref, hi_ref, xr_scr, xi_scr) = refs
    else:
        (u_ref, yin_ref, h0r_ref, h0i_ref, br_ref, bi_ref, cr_ref, ci_ref, tab_ref,
         y_ref, hr_ref, hi_ref, xr_scr, xi_scr) = refs
    tm = u_ref.shape[0]
    nblk = tm // SUBLANES
    cin = S5_WIDTH // S5_CHUNKS
    cst = S5_NSTATE // S5_CHUNKS

    @pl.when(pl.program_id(0) == 0)
    def _():
        hr_ref[...] = h0r_ref[...]
        hi_ref[...] = h0i_ref[...]

    for q in range(S5_CHUNKS):
        uq = u_ref[:, q * cin:(q + 1) * cin].astype(BF16)
        xr_scr[:, q * cst:(q + 1) * cst] = _bdot(uq, br_ref[q])
        xi_scr[:, q * cst:(q + 1) * cst] = _bdot(uq, bi_ref[q])

    edge = 0 if reverse else SUBLANES - 1
    for jc in range(S5_NSTATE // S5_LANE_CHUNK):
        sl = slice(jc * S5_LANE_CHUNK, (jc + 1) * S5_LANE_CHUNK)

        def body(r, carry, sl=sl):
            car, cai = carry
            blk = (nblk - 1 - r) if reverse else r
            row = pl.multiple_of(blk * SUBLANES, SUBLANES)
            xr = xr_scr[pl.ds(row, SUBLANES), sl]
            xi = xi_scr[pl.ds(row, SUBLANES), sl]
            for idx, dist in enumerate((1, 2, 4)):
                ar = tab_ref[2 * idx, :, sl]
                ai = tab_ref[2 * idx + 1, :, sl]
                shift = SUBLANES - dist if reverse else dist
                sr = pltpu.roll(xr, shift, 0)
                si = pltpu.roll(xi, shift, 0)
                xr, xi = xr + ar * sr - ai * si, xi + ar * si + ai * sr
            pr = tab_ref[6, :, sl]
            pi = tab_ref[7, :, sl]
            xr, xi = xr + pr * car - pi * cai, xi + pr * cai + pi * car
            xr_scr[pl.ds(row, SUBLANES), sl] = xr
            xi_scr[pl.ds(row, SUBLANES), sl] = xi
            shape = (SUBLANES, S5_LANE_CHUNK)
            return (jnp.broadcast_to(xr[edge:edge + 1, :], shape), jnp.broadcast_to(xi[edge:edge + 1, :], shape))

        car, cai = lax.fori_loop(0, nblk, body, (hr_ref[:, sl], hi_ref[:, sl]), unroll=2)
        hr_ref[:, sl] = car
        hi_ref[:, sl] = cai

    cout = S5_WIDTH // S5_CHUNKS
    for q in range(S5_CHUNKS):
        hr = xr_scr[:, q * cst:(q + 1) * cst].astype(BF16)
        hi = xi_scr[:, q * cst:(q + 1) * cst].astype(BF16)
        y = _bdot(hr, cr_ref[q]) + _bdot(hi, ci_ref[q])
        cs = slice(q * cout, (q + 1) * cout)
        if first:
            base = u_ref[:, cs] * d_ref[:, cs]
        else:
            base = yin_ref[:, cs]
        y_ref[:, cs] = base + y


def _s5_scan(u_src, extra, h0, prm, *, reverse, first):
    n = u_src.shape[0]
    tm = min(S5_TILE, n)
    nt = n // tm
    tmap = (lambda i: (nt - 1 - i, 0)) if reverse else (lambda i: (i, 0))
    full = lambda a: pl.BlockSpec(a.shape, lambda i: (0,) * a.ndim)
    extra_spec = full(extra) if first else pl.BlockSpec((tm, S5_WIDTH), tmap)
    st = jax.ShapeDtypeStruct((SUBLANES, S5_NSTATE), F32)
    st_spec = pl.BlockSpec((SUBLANES, S5_NSTATE), lambda i: (0, 0))
    return pl.pallas_call(
        functools.partial(_s5_kernel, reverse=reverse, first=first),
        grid=(nt,),
        in_specs=[pl.BlockSpec((tm, S5_WIDTH), tmap), extra_spec, st_spec, st_spec,
                  full(prm["br"]), full(prm["bi"]), full(prm["cr"]), full(prm["ci"]), full(prm["tab"])],
        out_specs=[pl.BlockSpec((tm, S5_WIDTH), tmap), st_spec, st_spec],
        out_shape=[jax.ShapeDtypeStruct((n, S5_WIDTH), F32), st, st],
        scratch_shapes=[pltpu.VMEM((tm, S5_NSTATE), F32), pltpu.VMEM((tm, S5_NSTATE), F32)],
        compiler_params=_cparams("arbitrary"),
        name="s5_scan_rev" if reverse else "s5_scan_fwd",
    )(u_src, extra, h0[0], h0[1], prm["br"], prm["bi"], prm["cr"], prm["ci"], prm["tab"])


def _s5_params(lam_re, lam_im, log_dt, b_re, b_im, c_re, c_im, reverse):
    dt = jnp.exp(log_dt)[:, None]
    mag = jnp.exp(lam_re * dt)
    a_re = mag * jnp.cos(lam_im * dt)
    a_im = mag * jnp.sin(lam_im * dt)
    den = lam_re * lam_re + lam_im * lam_im
    nr = a_re - 1.0
    k_re = (nr * lam_re + a_im * lam_im) / den
    k_im = (a_im * lam_re - nr * lam_im) / den
    bx_re = k_re[:, :, None] * b_re - k_im[:, :, None] * b_im
    bx_im = k_re[:, :, None] * b_im + k_im[:, :, None] * b_re
    gpc = S5_GROUPS // S5_CHUNKS
    eye = jnp.eye(gpc, dtype=F32)

    def bdiag_in(b):
        b = b.reshape(S5_CHUNKS, gpc, S5_STATE, S5_GROUP)
        return jnp.einsum('qgpc,gh->qgchp', b, eye).reshape(S5_CHUNKS, gpc * S5_GROUP, gpc * S5_STATE)

    def bdiag_out(c):
        c = c.reshape(S5_CHUNKS, gpc, S5_GROUP, S5_STATE)
        return jnp.einsum('qgcp,gh->qgphc', c, eye).reshape(S5_CHUNKS, gpc * S5_STATE, gpc * S5_GROUP)

    pr, pi = [a_re.reshape(-1)], [a_im.reshape(-1)]
    for _ in range(SUBLANES - 1):
        pr, pi = pr + [pr[-1] * pr[0] - pi[-1] * pi[0]], pi + [pr[-1] * pi[0] + pi[-1] * pr[0]]
    t = np.arange(SUBLANES)
    tabs = []
    for dist in (1, 2, 4):
        keep = (t <= SUBLANES - 1 - dist) if reverse else (t >= dist)
        mask = jnp.asarray(keep, F32)[:, None]
        tabs += [mask * pr[dist - 1][None, :], mask * pi[dist - 1][None, :]]
    order = [SUBLANES - 1 - i for i in range(SUBLANES)] if reverse else list(range(SUBLANES))
    tabs += [jnp.stack([pr[i] for i in order]), jnp.stack([pi[i] for i in order])]
    return {"br": bdiag_in(bx_re).astype(BF16), "bi": bdiag_in(bx_im).astype(BF16),
            "cr": bdiag_out(c_re).astype(BF16), "ci": bdiag_out(-c_im).astype(BF16),
            "tab": jnp.stack(tabs)}


def _gelu(x):
    return x * (0.5 * (1.0 + jnp.tanh(math.sqrt(2.0 / math.pi) * (x + 0.044715 * (x * x * x)))))


def _evout_kernel(x_ref, att_ref, y_ref, gt_ref, gw_ref, gb_ref, wa_ref, ws_ref, o_ref):
    g = _gelu(y_ref[...])
    z = _bdot(g.astype(BF16), gw_ref[...]) + gb_ref[...]
    ssm = g * jax.nn.sigmoid(z)
    o = _bdot(att_ref[...].astype(BF16), wa_ref[...]) + _bdot(ssm.astype(BF16), ws_ref[...])
    o_ref[...] = x_ref[...] + gt_ref[...] * o


def _even_out(x, att, y, gate, glu_w, glu_b, w_out):
    n, d = x.shape
    tm = min(512, n)
    wa = w_out[:MLA_HEADS * V_DIM].astype(BF16)
    ws = w_out[MLA_HEADS * V_DIM:].astype(BF16)
    gw = glu_w.astype(BF16)
    gb = glu_b.reshape(1, S5_WIDTH)
    full = lambda a: pl.BlockSpec(a.shape, lambda i: (0,) * a.ndim)
    return pl.pallas_call(
        _evout_kernel,
        grid=(n // tm,),
        in_specs=[pl.BlockSpec((tm, d), lambda i: (i, 0)),
                  pl.BlockSpec((tm, MLA_HEADS * V_DIM), lambda i: (i, 0)),
                  pl.BlockSpec((tm, S5_WIDTH), lambda i: (i, 0)),
                  full(gate), full(gw), full(gb), full(wa), full(ws)],
        out_specs=pl.BlockSpec((tm, d), lambda i: (i, 0)),
        out_shape=jax.ShapeDtypeStruct((n, d), F32),
        compiler_params=_cparams("parallel"),
        name="even_out",
    )(x, att, y, gate, gw, gb, wa, ws)


def _resid_kernel(x_ref, y_ref, gt_ref, w_ref, o_ref):
    o_ref[...] = x_ref[...] + gt_ref[...] * _bdot(y_ref[...].astype(BF16), w_ref[...])


def _resid_proj(x, y, gate, w):
    n, d = x.shape
    tm = min(512, n)
    wb = w.astype(BF16)
    full = lambda a: pl.BlockSpec(a.shape, lambda i: (0,) * a.ndim)
    return pl.pallas_call(
        _resid_kernel,
        grid=(n // tm,),
        in_specs=[pl.BlockSpec((tm, d), lambda i: (i, 0)), pl.BlockSpec((tm, y.shape[1]), lambda i: (i, 0)),
                  full(gate), full(wb)],
        out_specs=pl.BlockSpec((tm, d), lambda i: (i, 0)),
        out_shape=jax.ShapeDtypeStruct((n, d), F32),
        compiler_params=_cparams("parallel"),
        name="resid_proj",
    )(x, y, gate, wb)


def _sconv_kernel(z_ref, zp_ref, zn_ref, w_ref, b_ref, o_ref):
    i = pl.program_id(0)
    z = z_ref[...]
    tm = z.shape[0]
    rows = lax.broadcasted_iota(jnp.int32, z.shape, 0)
    prev_row = jnp.where(i > 0, zp_ref[SUBLANES - 1:SUBLANES, :], 0.0)
    next_row = jnp.where(i < pl.num_programs(0) - 1, zn_ref[0:1, :], 0.0)
    z_dn = jnp.where(rows == 0, prev_row, pltpu.roll(z, 1, 0))
    z_up = jnp.where(rows == tm - 1, next_row, pltpu.roll(z, tm - 1, 0))
    o_ref[...] = w_ref[0:1, :] * z_dn + w_ref[1:2, :] * z + w_ref[2:3, :] * z_up + b_ref[...]


def _short_conv(z, conv_w, conv_b):
    n, ch = z.shape
    tm = min(512, n)
    tc = 2048
    per = tm // SUBLANES
    last = n // SUBLANES - 1
    return pl.pallas_call(
        _sconv_kernel,
        grid=(n // tm, ch // tc),
        in_specs=[pl.BlockSpec((tm, tc), lambda i, j: (i, j)),
                  pl.BlockSpec((SUBLANES, tc), lambda i, j: (jnp.maximum(i * per - 1, 0), j)),
                  pl.BlockSpec((SUBLANES, tc), lambda i, j: (jnp.minimum((i + 1) * per, last), j)),
                  pl.BlockSpec((conv_w.shape[0], tc), lambda i, j: (0, j)),
                  pl.BlockSpec((1, tc), lambda i, j: (0, j))],
        out_specs=pl.BlockSpec((tm, tc), lambda i, j: (i, j)),
        out_shape=jax.ShapeDtypeStruct((n, ch), F32),
        compiler_params=_cparams("parallel", "parallel"),
        name="short_conv",
    )(z, z, z, conv_w, conv_b.reshape(1, ch))


def _filt_kernel(bands_ref, w1t_ref, w1c_ref, w1s_ref, b1_ref, w2_ref, b2_ref, w3_ref, b3_ref, fr_ref,
                 woh_ref, wol_ref, dl_ref, g_ref, s_ref, *, seq):
    i = pl.program_id(0)
    tl = g_ref.shape[0]
    m = i * tl + lax.broadcasted_iota(jnp.int32, (tl, 1), 0)
    lag = jnp.where(m < seq, m, 2 * seq - m)
    t = lag.astype(F32) / seq
    ang = ((2.0 * math.pi) * t) * bands_ref[...]
    fr = fr_ref[...]
    pre = t * w1t_ref[...] + _hdot(jnp.cos(ang), w1c_ref[...]) + _hdot(-jnp.sin(ang), w1s_ref[...]) + b1_ref[...]
    h = jnp.sin(fr * pre)
    h = jnp.sin(fr * (_hdot(h, w2_ref[...]) + b2_ref[...]))
    h = jnp.sin(fr * (_hdot(h, w3_ref[...]) + b3_ref[...]))
    f = _dot3(_split(h), (woh_ref[...], wol_ref[...])) * jnp.exp(-t * dl_ref[...])
    f = jnp.where(m == seq, 0.0, f)
    g_ref[...] = f

    @pl.when(i == 0)
    def _():
        s_ref[...] = jnp.zeros_like(s_ref)

    s_ref[...] += jnp.sum(jnp.abs(f), axis=0, keepdims=True)


def _hyena_filter(seq, w1, b1, w2, b2, w3, b3, freq, w_out_o):
    fh = HY_FILT_HIDDEN
    w = HY_WIDTH
    tl = min(512, seq)
    pad2 = lambda a: jnp.pad(a, ((0, LANES - a.shape[0]), (0, LANES - a.shape[1])))
    padv = lambda a: _pad_lanes(a.reshape(1, -1))
    bands = padv(jnp.linspace(1e-4, HY_BANDS - 1, HY_BANDS, dtype=F32))
    w1t = padv(w1[0])
    w1c = pad2(w1[1:1 + HY_BANDS])
    w1s = pad2(w1[1 + HY_BANDS:])
    wo_hi, wo_lo = _split(jnp.pad(w_out_o.transpose(1, 0, 2), ((0, 0), (0, LANES - fh), (0, 0))))
    deltas = jnp.abs(jnp.linspace(HY_DECAY_MIN, HY_DECAY_MAX, w, dtype=F32)).reshape(1, w)
    nt = 2 * seq // tl
    half = seq // tl
    full = lambda a: pl.BlockSpec(a.shape, lambda i: (0,) * a.ndim)
    args = (bands, w1t, w1c, w1s, padv(b1), pad2(w2), padv(b2), pad2(w3), padv(b3), padv(freq))
    return pl.pallas_call(
        functools.partial(_filt_kernel, seq=seq),
        grid=(nt,),
        in_specs=[full(a) for a in args] + [pl.BlockSpec((None, LANES, w), lambda i: (i // half, 0, 0)),
                                            pl.BlockSpec((None, LANES, w), lambda i: (i // half, 0, 0)),
                                            full(deltas)],
        out_specs=[pl.BlockSpec((tl, w), lambda i: (i, 0)), pl.BlockSpec((1, w), lambda i: (0, 0))],
        out_shape=[jax.ShapeDtypeStruct((2 * seq, w), F32), jax.ShapeDtypeStruct((1, w), F32)],
        compiler_params=_cparams("arbitrary"),
        name="hyena_filter",
    )(*args, wo_hi, wo_lo, deltas)


def _dft_consts(seq):
    n = 2 * seq
    n2 = DFT_N2
    n1 = n // n2
    k1 = n1 // 2 + 1
    two_pi = 2.0 * np.pi
    kk, nn = np.meshgrid(np.arange(k1), np.arange(n1), indexing="ij")
    ang_a = two_pi * ((kk * nn) % n1) / n1
    fa = (np.cos(ang_a), -np.sin(ang_a))
    k2, m2 = np.meshgrid(np.arange(n2), np.arange(n2), indexing="ij")
    ang_c = two_pi * ((k2 * m2) % n2) / n2
    fc = (np.cos(ang_c), -np.sin(ang_c))
    kk, mm = np.meshgrid(np.arange(k1), np.arange(n2), indexing="ij")
    ang_t = two_pi * ((kk * mm) % n) / n
    tw = (np.cos(ang_t), -np.sin(ang_t))
    weight = np.where((np.arange(k1) == 0) | (np.arange(k1) == n1 // 2), 1.0, 2.0) / n
    rows, cols = np.meshgrid(np.arange(n1 // 2), np.arange(k1), indexing="ij")
    ang_f = two_pi * ((rows * cols) % n1) / n1
    mc = np.cos(ang_f) * weight[None, :]
    ms = -np.sin(ang_f) * weight[None, :]
    kp = -(-k1 // SUBLANES) * SUBLANES
    rpad = lambda a: np.pad(a, ((0, kp - k1), (0, 0)))
    cpad = lambda a: np.pad(a, ((0, 0), (0, kp - k1)))
    f32 = lambda a: jnp.asarray(a, F32)
    return {"n1": n1, "k1": k1, "kp": kp,
            "fa_r": f32(rpad(fa[0])), "fa_i": f32(rpad(fa[1])), "fc_r": f32(fc[0]), "fc_i": f32(fc[1]),
            "twa_r": f32(rpad(tw[0]).T[:, :, None]), "twa_i": f32(rpad(tw[1]).T[:, :, None]),
            "twc_r": f32(rpad(tw[0])[:, :, None]), "twc_i": f32(rpad(tw[1])[:, :, None]),
            "mc": f32(cpad(mc)), "ms": f32(cpad(ms))}


def _dfta_kernel(x_ref, fr_ref, fi_ref, twr_ref, twi_ref, or_ref, oi_ref):
    fr = _split(fr_ref[...])
    fi = _split(fi_ref[...])
    xt = pltpu.einshape("nsw->snw", x_ref[...])
    out_r, out_i = [], []
    for s in range(SUBLANES):
        xs = _split(xt[s])
        ar = _dot3(fr, xs)
        ai = _dot3(fi, xs)
        tr = twr_ref[s]
        ti = twi_ref[s]
        out_r.append(ar * tr - ai * ti)
        out_i.append(ar * ti + ai * tr)
    or_ref[...] = pltpu.einshape("skw->ksw", jnp.stack(out_r))
    oi_ref[...] = pltpu.einshape("skw->ksw", jnp.stack(out_i))


def _dft_stage_a(x3, col0, dc, n1c):
    k1 = dc["kp"]
    tw = 1024
    wt = HY_WIDTH // tw
    fr, fi = dc["fa_r"][:, :n1c], dc["fa_i"][:, :n1c]
    out = jax.ShapeDtypeStruct((k1, DFT_N2, HY_WIDTH), F32)
    return pl.pallas_call(
        _dfta_kernel,
        grid=(DFT_N2 // SUBLANES, wt),
        in_specs=[pl.BlockSpec((n1c, SUBLANES, tw), lambda b, j: (0, b, col0 * wt + j)),
                  pl.BlockSpec((k1, n1c), lambda b, j: (0, 0)),
                  pl.BlockSpec((k1, n1c), lambda b, j: (0, 0)),
                  pl.BlockSpec((SUBLANES, k1, 1), lambda b, j: (b, 0, 0)),
                  pl.BlockSpec((SUBLANES, k1, 1), lambda b, j: (b, 0, 0))],
        out_specs=[pl.BlockSpec((k1, SUBLANES, tw), lambda b, j: (0, b, j))] * 2,
        out_shape=[out, out],
        compiler_params=_cparams("parallel", "parallel"),
        name="dft_stage_a",
    )(x3, fr, fi, dc["twa_r"], dc["twa_i"])


def _dftc_filt_kernel(ar_ref, ai_ref, fr_ref, fi_ref, s_ref, gr_ref, gi_ref, *, k1):
    @pl.when(pl.program_id(0) < k1)
    def _():
        ar = _split(ar_ref[...])
        ai = _split(ai_ref[...])
        fr = _split(fr_ref[...])
        fi = _split(fi_ref[...])
        inv = 1.0 / s_ref[...]
        gr_ref[...] = (_dot3(fr, ar) - _dot3(fi, ai)) * inv
        gi_ref[...] = (_dot3(fr, ai) + _dot3(fi, ar)) * inv

    @pl.when(pl.program_id(0) >= k1)
    def _():
        gr_ref[...] = jnp.zeros_like(gr_ref)
        gi_ref[...] = jnp.zeros_like(gi_ref)


def _dft_filter_spectrum(ar, ai, abs_sum, dc):
    kp = dc["kp"]
    tw = 1024
    blk = pl.BlockSpec((None, DFT_N2, tw), lambda k, j: (k, 0, j))
    mat = pl.BlockSpec((DFT_N2, DFT_N2), lambda k, j: (0, 0))
    out = jax.ShapeDtypeStruct((kp, DFT_N2, HY_WIDTH), F32)
    return pl.pallas_call(
        functools.partial(_dftc_filt_kernel, k1=dc["k1"]),
        grid=(kp, HY_WIDTH // tw),
        in_specs=[blk, blk, mat, mat, pl.BlockSpec((1, tw), lambda k, j: (0, j))],
        out_specs=[blk, blk],
        out_shape=[out, out],
        compiler_params=_cparams("parallel", "parallel"),
        name="dft_filter_spectrum",
    )(ar, ai, dc["fc_r"], dc["fc_i"], abs_sum)


def _dftmid_kernel(ar_ref, ai_ref, gr_ref, gi_ref, fr_ref, fi_ref, twr_ref, twi_ref, or_ref, oi_ref, *, k1):
    @pl.when(pl.program_id(0) < k1)
    def _():
        ar = _split(ar_ref[...])
        ai = _split(ai_ref[...])
        fr = _split(fr_ref[...])
        fi = _split(fi_ref[...])
        xr = _dot3(fr, ar) - _dot3(fi, ai)
        xi = _dot3(fr, ai) + _dot3(fi, ar)
        gr = gr_ref[...]
        gi = gi_ref[...]
        yr = _split(xr * gr - xi * gi)
        yi = _split(xr * gi + xi * gr)
        br = _dot3(fr, yr) + _dot3(fi, yi)
        bi = _dot3(fr, yi) - _dot3(fi, yr)
        tr = twr_ref[...]
        ti = twi_ref[...]
        or_ref[...] = br * tr + bi * ti
        oi_ref[...] = bi * tr - br * ti

    @pl.when(pl.program_id(0) >= k1)
    def _():
        or_ref[...] = jnp.zeros_like(or_ref)
        oi_ref[...] = jnp.zeros_like(oi_ref)


def _dft_middle(ar, ai, gr, gi, dc):
    kp = dc["kp"]
    tw = 1024
    blk = pl.BlockSpec((None, DFT_N2, tw), lambda k, j: (k, 0, j))
    mat = pl.BlockSpec((DFT_N2, DFT_N2), lambda k, j: (0, 0))
    twd = pl.BlockSpec((None, DFT_N2, 1), lambda k, j: (k, 0, 0))
    out = jax.ShapeDtypeStruct((kp, DFT_N2, HY_WIDTH), F32)
    return pl.pallas_call(
        functools.partial(_dftmid_kernel, k1=dc["k1"]),
        grid=(kp, HY_WIDTH // tw),
        in_specs=[blk, blk, blk, blk, mat, mat, twd, twd],
        out_specs=[blk, blk],
        out_shape=[out, out],
        compiler_params=_cparams("parallel", "parallel"),
        name="dft_middle",
    )(ar, ai, gr, gi, dc["fc_r"], dc["fc_i"], dc["twc_r"], dc["twc_i"])


def _dftf_kernel(br_ref, bi_ref, mc_ref, ms_ref, gate_ref, y_ref, skip_ref, o_ref):
    mc = _split(mc_ref[...])
    ms = _split(ms_ref[...])
    brt = pltpu.einshape("ksw->skw", br_ref[...])
    bit = pltpu.einshape("ksw->skw", bi_ref[...])
    conv = [_dot3(mc, _split(brt[s])) + _dot3(ms, _split(bit[s])) for s in range(SUBLANES)]
    conv = pltpu.einshape("snw->nsw", jnp.stack(conv))
    o_ref[...] = gate_ref[...] * (conv + skip_ref[...] * y_ref[...])


def _dft_final(br, bi, dc, gate3, gate_col, y3, y_col, skip):
    k1 = dc["kp"]
    nh = dc["n1"] // 2
    tw = 1024
    wt = HY_WIDTH // tw
    bblk = pl.BlockSpec((k1, SUBLANES, tw), lambda b, j: (0, b, j))
    mat = pl.BlockSpec((nh, k1), lambda b, j: (0, 0))
    return pl.pallas_call(
        _dftf_kernel,
        grid=(DFT_N2 // SUBLANES, wt),
        in_specs=[bblk, bblk, mat, mat,
                  pl.BlockSpec((nh, SUBLANES, tw), lambda b, j: (0, b, gate_col * wt + j)),
                  pl.BlockSpec((nh, SUBLANES, tw), lambda b, j: (0, b, y_col * wt + j)),
                  pl.BlockSpec((1, tw), lambda b, j: (0, j))],
        out_specs=pl.BlockSpec((nh, SUBLANES, tw), lambda b, j: (0, b, j)),
        out_shape=jax.ShapeDtypeStruct((nh, DFT_N2, HY_WIDTH), F32),
        compiler_params=_cparams("parallel", "parallel"),
        name="dft_final",
    )(br, bi, dc["mc"], dc["ms"], gate3, y3, skip)


def _hyena_mixer(x, shift, scale, gate, g_norm, w_in, conv_w, conv_b, w1, b1, w2, b2, w3, b3, freq,
                 filt_w_out, skip, w_out):
    seq = x.shape[0]
    dc = _dft_consts(seq)
    nh = dc["n1"] // 2
    z = _norm_proj(x, shift, scale, g_norm, w_in)
    zc = _short_conv(z, conv_w, conv_b)
    zc3 = zc.reshape(nh, DFT_N2, 3 * HY_WIDTH)
    y3, y_col = zc3, 2
    for o in range(2):
        g, abs_sum = _hyena_filter(seq, w1, b1, w2, b2, w3, b3, freq, filt_w_out[:, o])
        ga_r, ga_i = _dft_stage_a(g.reshape(dc["n1"], DFT_N2, HY_WIDTH), 0, dc, dc["n1"])
        gr, gi = _dft_filter_spectrum(ga_r, ga_i, abs_sum, dc)
        ya_r, ya_i = _dft_stage_a(y3, y_col, dc, nh)
        br, bi = _dft_middle(ya_r, ya_i, gr, gi, dc)
        y3 = _dft_final(br, bi, dc, zc3, o, y3, y_col, skip[o].reshape(1, HY_WIDTH))
        y_col = 0
    return _resid_proj(x, y3.reshape(seq, HY_WIDTH), gate, w_out)


def _even_mixer(x, cx, mx, mc, g_norm, ev_w_in, q_a_norm_g, w_uq, kv_a_norm_g, w_ukv, q_head_g, k_head_g,
                lam_re, lam_im, log_dt, b_re, b_im, c_re, c_im, d_skip, glu_w, glu_b, ev_w_out):
    n = x.shape[0]
    nc = cx.shape[0]
    cuts = (Q_LORA, Q_LORA + KV_LORA, Q_LORA + KV_LORA + QK_ROPE)
    w_qa, w_kva, w_kpe, w_u = (ev_w_in[:, :cuts[0]], ev_w_in[:, cuts[0]:cuts[1]],
                               ev_w_in[:, cuts[1]:cuts[2]], ev_w_in[:, cuts[2]:])
    w_all = jnp.concatenate([w_u, w_qa, w_kva, _pad_lanes(w_kpe), _pad_lanes(w_kpe[:, _ROPE_SWAP])], axis=1)
    proj_x = _norm_proj(x, mx[3], mx[4], g_norm, w_all)
    proj_c = _norm_proj(cx, mc[3], mc[4], g_norm, w_all)

    wq3, wkv, gq3, gk3 = _mla_weights(w_uq, w_ukv, q_head_g, k_head_g)
    gqa = q_a_norm_g.reshape(1, Q_LORA)
    gkva = kv_a_norm_g.reshape(1, KV_LORA)
    cos_x, sin_x = _rope_tables(n)
    cos_c = _pad_lanes(jnp.ones((nc, QK_ROPE), F32))
    sin_c = jnp.zeros((nc, LANES), F32)
    q_x, k_x, v_x = _mla_heads(proj_x, cos_x, sin_x, gqa, gkva, wq3, wkv, gq3, gk3)
    _, k_c, v_c = _mla_heads(proj_c, cos_c, sin_c, gqa, gkva, wq3, wkv, gq3, gk3)
    att = _attention(q_x, jnp.concatenate([k_x, k_c], axis=1), jnp.concatenate([v_x, v_c], axis=1))

    d_row = d_skip.reshape(1, S5_WIDTH)
    zero = jnp.zeros((SUBLANES, S5_NSTATE), F32)
    y = None
    for direction, reverse in ((0, False), (1, True)):
        prm = _s5_params(lam_re[direction], lam_im[direction], log_dt[direction], b_re[direction],
                         b_im[direction], c_re[direction], c_im[direction], reverse)
        _, hc_r, hc_i = _s5_scan(proj_c, d_row, (zero, zero), prm, reverse=reverse, first=True)
        if y is None:
            y, _, _ = _s5_scan(proj_x, d_row, (hc_r, hc_i), prm, reverse=reverse, first=True)
        else:
            y, _, _ = _s5_scan(proj_x, y, (hc_r, hc_i), prm, reverse=reverse, first=False)
    return _even_out(x, att, y, mx[5], glu_w, glu_b, ev_w_out)


def kernel(x, c, ctx, c_ctx, ada_w, ada_b, norm_g, ffn_w_in, ffn_w_out, ev_w_in, mla_q_a_norm_g, mla_w_uq, mla_kv_a_norm_g, mla_w_ukv, mla_q_head_g, mla_k_head_g, s5_lam_re, s5_lam_im, s5_log_dt, s5_b_re, s5_b_im, s5_c_re, s5_c_im, s5_d, s5_glu_w, s5_glu_b, ev_w_out, hy_w_in, hy_conv_w, hy_conv_b, hy_filt_w1, hy_filt_b1, hy_filt_w2, hy_filt_b2, hy_filt_w3, hy_filt_b3, hy_filt_freq, hy_filt_w_out, hy_skip, hy_w_out):
    assert x.shape[0] == 1 and ctx.shape[0] == 1, "kernels are written for batch 1"
    depth = ada_w.shape[0]
    mods = _ada_mods(c, c_ctx, ada_w, ada_b)
    xs = x[0]
    cx = ctx[0]
    for i in range(depth):
        even = i % 2 == 0
        li = i // 2
        need_after = any(j % 2 == 0 for j in range(i + 1, depth))
        use_ctx = even or need_after
        assert not need_after, "context outputs after an even layer are not needed at this depth"
        mx = mods[i, 0]
        mc = mods[i, 1]
        xs = _ffn_half(xs, mx, 0, norm_g[i, 0], ffn_w_in[i, 0], ffn_w_out[i, 0])
        if use_ctx:
            cx = _ffn_half(cx, mc, 0, norm_g[i, 0], ffn_w_in[i, 0], ffn_w_out[i, 0])
        if even:
            xs = _even_mixer(xs, cx, mx, mc, norm_g[i, 1], ev_w_in[li], mla_q_a_norm_g[li], mla_w_uq[li],
                             mla_kv_a_norm_g[li], mla_w_ukv[li], mla_q_head_g[li], mla_k_head_g[li],
                             s5_lam_re[li], s5_lam_im[li], s5_log_dt[li], s5_b_re[li], s5_b_im[li],
                             s5_c_re[li], s5_c_im[li], s5_d[li], s5_glu_w[li], s5_glu_b[li], ev_w_out[li])
        else:
            xs = _hyena_mixer(xs, mx[3], mx[4], mx[5], norm_g[i, 1], hy_w_in[li], hy_conv_w[li], hy_conv_b[li],
                              hy_filt_w1[li], hy_filt_b1[li], hy_filt_w2[li], hy_filt_b2[li], hy_filt_w3[li],
                              hy_filt_b3[li], hy_filt_freq[li], hy_filt_w_out[li], hy_skip[li], hy_w_out[li])
        xs = _ffn_half(xs, mx, 6, norm_g[i, 2], ffn_w_in[i, 1], ffn_w_out[i, 1])
    return xs[None]
```

```python
import functools
import math

import numpy as np
import jax
import jax.numpy as jnp
from jax import lax
from jax.experimental import pallas as pl
from jax.experimental.pallas import tpu as pltpu

F32 = jnp.float32
BF16 = jnp.bfloat16
HIGHEST = lax.Precision.HIGHEST

D_MODEL = 2048
N_MOD = 9
D_FF = 5632
NORM_EPS = 1e-6
GRID_W = 64
MLA_HEADS = 8
QK_NOPE = 128
QK_ROPE = 64
QK_DIM = QK_NOPE + QK_ROPE
V_DIM = 128
Q_LORA = 512
KV_LORA = 256
ROPE_BASE = 10000.0
S5_WIDTH = 1024
S5_GROUP = 16
S5_GROUPS = S5_WIDTH // S5_GROUP
S5_STATE = 64
S5_NSTATE = S5_GROUPS * S5_STATE
S5_CHUNKS = 4
HY_WIDTH = D_MODEL
HY_BANDS = 16
HY_FILT_HIDDEN = 64
HY_DECAY_MIN = math.log(1e-2) / 1.5
HY_DECAY_MAX = math.log(1e-2) / 0.3
DFT_N2 = 256
FILT_PACK = 8
LANES = 128
SUBLANES = 8
VMEM_LIMIT = 56 * 1024 * 1024


def _cparams(*sem):
    return pltpu.CompilerParams(dimension_semantics=sem, vmem_limit_bytes=VMEM_LIMIT)


def _hdot(a, b):
    return jnp.dot(a, b, preferred_element_type=F32, precision=HIGHEST)


def _bdot(a, b):
    return jnp.dot(a, b, preferred_element_type=F32)


def _split(x):
    hi = x.astype(BF16)
    return hi, (x - hi.astype(F32)).astype(BF16)


def _dot3(a, b):
    return _bdot(a[0], b[0]) + (_bdot(a[1], b[0]) + _bdot(a[0], b[1]))


def _lhs3(a):
    hi, lo = _split(a)
    return jnp.concatenate([hi, lo, hi], axis=1)


def _rhs3(b):
    hi, lo = _split(b)
    return jnp.concatenate([hi, hi, lo], axis=0)


def _norm_mod(x, g, scale, shift):
    y = x * lax.rsqrt(jnp.mean(x * x, axis=-1, keepdims=True) + NORM_EPS) * g
    return y * (1.0 + scale) + shift


def _mods_kernel(s_ref, w_ref, b_ref, o_ref):
    s = s_ref[...]
    s = s * jax.nn.sigmoid(s)
    o_ref[...] = _hdot(s, w_ref[...]) + b_ref[...]


def _ada_mods(c, c_ctx, ada_w, ada_b):
    depth, d, nd = ada_w.shape
    s = jnp.zeros((SUBLANES, d), F32).at[0].set(c[0]).at[1].set(c_ctx)
    tn = 1024
    out = pl.pallas_call(
        _mods_kernel,
        grid=(depth, nd // tn),
        in_specs=[pl.BlockSpec((SUBLANES, d), lambda l, j: (0, 0)),
                  pl.BlockSpec((None, d, tn), lambda l, j: (l, 0, j)),
                  pl.BlockSpec((None, 1, tn), lambda l, j: (l, 0, j))],
        out_specs=pl.BlockSpec((None, SUBLANES, tn), lambda l, j: (l, 0, j)),
        out_shape=jax.ShapeDtypeStruct((depth, SUBLANES, nd), F32),
        compiler_params=_cparams("parallel", "parallel"),
        name="ada_mods",
    )(s, ada_w, ada_b.reshape(depth, 1, nd))
    return out[:, :2].reshape(depth, 2, N_MOD, 1, d)


def _ffn_kernel(x_ref, sh_ref, sc_ref, gt_ref, g_ref, wg_ref, wu_ref, wo_ref, o_ref, h_scr, acc_scr):
    j = pl.program_id(1)

    @pl.when(j == 0)
    def _():
        h_scr[...] = _norm_mod(x_ref[...], g_ref[...], sc_ref[...], sh_ref[...]).astype(BF16)
        acc_scr[...] = jnp.zeros_like(acc_scr)

    h = h_scr[...]
    g = _bdot(h, wg_ref[...])
    u = _bdot(h, wu_ref[...])
    a = (g * jax.nn.sigmoid(g)) * u
    acc_scr[...] += _bdot(a.astype(BF16), wo_ref[...])

    @pl.when(j == pl.num_programs(1) - 1)
    def _():
        o_ref[...] = x_ref[...] + gt_ref[...] * (0.5 * acc_scr[...])


def _ffn_half(x, mod, k0, g_norm, w_in, w_out):
    n, d = x.shape
    f = w_out.shape[0]
    tm = min(512, n)
    tf = 512
    nf = f // tf
    vec = pl.BlockSpec((1, d), lambda i, j: (0, 0))
    return pl.pallas_call(
        _ffn_kernel,
        grid=(n // tm, nf),
        in_specs=[pl.BlockSpec((tm, d), lambda i, j: (i, 0)), vec, vec, vec, vec,
                  pl.BlockSpec((d, tf), lambda i, j: (0, j)),
                  pl.BlockSpec((d, tf), lambda i, j: (0, j + nf)),
                  pl.BlockSpec((tf, d), lambda i, j: (j, 0))],
        out_specs=pl.BlockSpec((tm, d), lambda i, j: (i, 0)),
        out_shape=jax.ShapeDtypeStruct((n, d), F32),
        scratch_shapes=[pltpu.VMEM((tm, d), BF16), pltpu.VMEM((tm, d), F32)],
        compiler_params=_cparams("parallel", "arbitrary"),
        name="ffn_half",
    )(x, mod[k0], mod[k0 + 1], mod[k0 + 2], g_norm.reshape(1, d),
      w_in.astype(BF16), w_in.astype(BF16), w_out.astype(BF16))


def _proj_kernel(x_ref, sh_ref, sc_ref, g_ref, w_ref, o_ref, h_scr):
    @pl.when(pl.program_id(1) == 0)
    def _():
        h_scr[...] = _norm_mod(x_ref[...], g_ref[...], sc_ref[...], sh_ref[...]).astype(BF16)

    o_ref[...] = _bdot(h_scr[...], w_ref[...])


def _norm_proj(x, shift, scale, g_norm, w):
    n, d = x.shape
    nout = w.shape[1]
    tm = min(512, n)
    tn = 2048
    vec = pl.BlockSpec((1, d), lambda i, j: (0, 0))
    return pl.pallas_call(
        _proj_kernel,
        grid=(n // tm, nout // tn),
        in_specs=[pl.BlockSpec((tm, d), lambda i, j: (i, 0)), vec, vec, vec,
                  pl.BlockSpec((d, tn), lambda i, j: (0, j))],
        out_specs=pl.BlockSpec((tm, tn), lambda i, j: (i, j)),
        out_shape=jax.ShapeDtypeStruct((n, nout), F32),
        scratch_shapes=[pltpu.VMEM((tm, d), BF16)],
        compiler_params=_cparams("parallel", "arbitrary"),
        name="norm_proj",
    )(x, shift, scale, g_norm.reshape(1, d), w.astype(BF16))


def _mla_kernel(qa_ref, kva_ref, kpe_ref, cos_ref, sin_ref, gqa_ref, gkva_ref, wq_ref, wkv_ref,
                gq_ref, gk_ref, q_ref, k_ref, v_ref):
    def rms(t, g):
        return t * lax.rsqrt(jnp.mean(t * t, axis=-1, keepdims=True) + NORM_EPS) * g

    qn = rms(qa_ref[...], gqa_ref[...]).astype(BF16)
    kvn = rms(kva_ref[...], gkva_ref[...]).astype(BF16)
    cos = cos_ref[...]
    sin = sin_ref[...]
    gq = gq_ref[...]
    gk = gk_ref[...]
    kpe = kpe_ref[:, :LANES]
    kpe_sw = kpe_ref[:, LANES:]
    kpe_ss = jnp.sum(kpe * kpe, axis=-1, keepdims=True)
    k_rot = (kpe * gk[:, LANES:2 * LANES]) * cos + (kpe_sw * gk[:, 2 * LANES:]) * sin
    inv_dim = 1.0 / QK_DIM
    q_scale = math.log2(math.e) / math.sqrt(QK_DIM)
    for h in range(MLA_HEADS):
        qh = _bdot(qn, wq_ref[h])
        nope, pe, pe_sw = qh[:, :LANES], qh[:, LANES:2 * LANES], qh[:, 2 * LANES:]
        ss = jnp.sum(nope * nope, axis=-1, keepdims=True) + jnp.sum(pe * pe, axis=-1, keepdims=True)
        r = lax.rsqrt(ss * inv_dim + NORM_EPS) * q_scale
        q_rot = (pe * gq[:, LANES:2 * LANES]) * cos + (pe_sw * gq[:, 2 * LANES:]) * sin
        q_ref[h, :, :LANES] = (nope * r * gq[:, :LANES]).astype(BF16)
        q_ref[h, :, LANES:] = (q_rot * r).astype(BF16)
        kv = _bdot(kvn, wkv_ref[h])
        k_nope, vv = kv[:, :LANES], kv[:, LANES:]
        ssk = jnp.sum(k_nope * k_nope, axis=-1, keepdims=True) + kpe_ss
        rk = lax.rsqrt(ssk * inv_dim + NORM_EPS)
        k_ref[h, :, :LANES] = (k_nope * rk * gk[:, :LANES]).astype(BF16)
        k_ref[h, :, LANES:] = (k_rot * rk).astype(BF16)
        v_ref[h] = vv.astype(BF16)


_ROPE_SWAP = np.concatenate([np.arange(16, 32), np.arange(0, 16), np.arange(48, 64), np.arange(32, 48)])


def _pad_lanes(a, width=LANES):
    return jnp.pad(a, [(0, 0)] * (a.ndim - 1) + [(0, width - a.shape[-1])])


def _mla_weights(w_uq, w_ukv, q_head_g, k_head_g):
    wq = w_uq.reshape(Q_LORA, MLA_HEADS, QK_DIM).transpose(1, 0, 2)
    wq_pe = wq[:, :, QK_NOPE:]
    wq3 = jnp.concatenate([wq[:, :, :QK_NOPE], _pad_lanes(wq_pe), _pad_lanes(wq_pe[:, :, _ROPE_SWAP])], axis=-1)
    wkv = w_ukv.reshape(KV_LORA, MLA_HEADS, QK_NOPE + V_DIM).transpose(1, 0, 2)

    def g3(g):
        pe = g[QK_NOPE:]
        return jnp.concatenate([g[:QK_NOPE], _pad_lanes(pe), _pad_lanes(pe[_ROPE_SWAP])]).reshape(1, 3 * LANES)

    return wq3.astype(BF16), wkv.astype(BF16), g3(q_head_g), g3(k_head_g)


def _rope_tables(n):
    row = jnp.repeat(jnp.arange(n // GRID_W, dtype=F32), GRID_W)
    col = jnp.tile(jnp.arange(GRID_W, dtype=F32), n // GRID_W)
    n_freq = QK_ROPE // 4
    inv_freq = ROPE_BASE ** (-jnp.arange(n_freq, dtype=F32) / n_freq)
    ang_r = row[:, None] * inv_freq
    ang_c = col[:, None] * inv_freq
    cr, sr, cc, sc = jnp.cos(ang_r), jnp.sin(ang_r), jnp.cos(ang_c), jnp.sin(ang_c)
    cos = jnp.concatenate([cr, cr, cc, cc], axis=-1)
    sin = jnp.concatenate([-sr, sr, -sc, sc], axis=-1)
    return _pad_lanes(cos), _pad_lanes(sin)


def _mla_heads(proj, cos, sin, gqa, gkva, wq3, wkv, gq3, gk3):
    n = proj.shape[0]
    tm = min(512, n)
    hd = 2 * LANES
    full = lambda a: pl.BlockSpec(a.shape, lambda i: (0,) * a.ndim)
    return pl.pallas_call(
        _mla_kernel,
        grid=(n // tm,),
        in_specs=[pl.BlockSpec((tm, Q_LORA), lambda i: (i, S5_WIDTH // Q_LORA)),
                  pl.BlockSpec((tm, KV_LORA), lambda i: (i, (S5_WIDTH + Q_LORA) // KV_LORA)),
                  pl.BlockSpec((tm, 2 * LANES), lambda i: (i, (S5_WIDTH + Q_LORA + KV_LORA) // (2 * LANES))),
                  pl.BlockSpec((tm, LANES), lambda i: (i, 0)),
                  pl.BlockSpec((tm, LANES), lambda i: (i, 0)),
                  full(gqa), full(gkva), full(wq3), full(wkv), full(gq3), full(gk3)],
        out_specs=[pl.BlockSpec((MLA_HEADS, tm, hd), lambda i: (0, i, 0)),
                   pl.BlockSpec((MLA_HEADS, tm, hd), lambda i: (0, i, 0)),
                   pl.BlockSpec((MLA_HEADS, tm, V_DIM), lambda i: (0, i, 0))],
        out_shape=[jax.ShapeDtypeStruct((MLA_HEADS, n, hd), BF16),
                   jax.ShapeDtypeStruct((MLA_HEADS, n, hd), BF16),
                   jax.ShapeDtypeStruct((MLA_HEADS, n, V_DIM), BF16)],
        compiler_params=_cparams("parallel"),
        name="mla_heads",
    )(proj, proj, proj, cos, sin, gqa, gkva, wq3, wkv, gq3, gk3)


def _attn_kernel(q_ref, k_ref, v_ref, o_ref, s_scr, bm_scr, m_scr, acc_scr, *, nkv):
    t = pl.program_id(0)
    tk = k_ref.shape[0]
    jp = lax.rem(t + (nkv - 1), nkv)

    @pl.when(t == 0)
    def _():
        s_scr[1] = jnp.zeros(s_scr.shape[1:], F32)
        bm_scr[1] = jnp.zeros(bm_scr.shape[1:], F32)
        m_scr[...] = jnp.zeros_like(m_scr)
        acc_scr[...] = jnp.zeros_like(acc_scr)

    def step(cur, prev):
        s_new = lax.dot_general(q_ref[...], k_ref[...], (((1,), (1,)), ((), ())), preferred_element_type=F32)
        s_scr[cur] = s_new
        bm_scr[cur] = jnp.max(s_new, axis=-1, keepdims=True)
        m_prev = jnp.where(jp == 0, -jnp.inf, m_scr[...])
        m_new = jnp.maximum(m_prev, bm_scr[prev])
        alpha = jnp.exp2(m_prev - m_new)
        p = jnp.exp2(s_scr[prev] - m_new).astype(BF16)
        v_ext = jnp.concatenate([v_ref[...], jnp.ones((tk, LANES), BF16)], axis=1)
        acc_scr[...] = alpha * acc_scr[...] + _bdot(p, v_ext)
        m_scr[...] = m_new

    @pl.when(lax.rem(t, 2) == 0)
    def _():
        step(0, 1)

    @pl.when(lax.rem(t, 2) == 1)
    def _():
        step(1, 0)

    @pl.when((jp == nkv - 1) & (t > 0))
    def _():
        acc = acc_scr[...]
        o_ref[...] = acc[:, :V_DIM] / acc[:, V_DIM:]


def _kv_tile(m):
    for t in (1280, 1024, 768, 512, 256, 128):
        if m % t == 0:
            return t
    raise ValueError(f"key length {m} is not a multiple of {LANES}")


def _attention(q, k, v):
    h, n, hd = q.shape
    m = k.shape[1]
    tq = min(1024, n)
    tk = _kv_tile(m)
    nq = n // tq
    nkv = m // tk
    total = h * nq * nkv

    def qk_idx(t):
        tt = jnp.minimum(t, total - 1)
        return tt // (nq * nkv), (tt // nkv) % nq, tt % nkv

    def pv_idx(t):
        tt = jnp.maximum(t - 1, 0)
        return tt // (nq * nkv), (tt // nkv) % nq, tt % nkv

    return pl.pallas_call(
        functools.partial(_attn_kernel, nkv=nkv),
        grid=(total + 1,),
        in_specs=[pl.BlockSpec((None, tq, hd), lambda t: (qk_idx(t)[0], qk_idx(t)[1], 0)),
                  pl.BlockSpec((None, tk, hd), lambda t: (qk_idx(t)[0], qk_idx(t)[2], 0)),
                  pl.BlockSpec((None, tk, V_DIM), lambda t: (pv_idx(t)[0], pv_idx(t)[2], 0))],
        out_specs=pl.BlockSpec((tq, V_DIM), lambda t: (pv_idx(t)[1], pv_idx(t)[0])),
        out_shape=jax.ShapeDtypeStruct((n, h * V_DIM), F32),
        scratch_shapes=[pltpu.VMEM((2, tq, tk), F32), pltpu.VMEM((2, tq, 1), F32),
                        pltpu.VMEM((tq, 1), F32), pltpu.VMEM((tq, V_DIM + LANES), F32)],
        compiler_params=_cparams("arbitrary"),
        name="flash_attention",
    )(q, k, v)


S5_TILE = 256
S5_LANE_CHUNK = 512


def _s5_kernel(*refs, reverse, first):
    if first:
        (u_ref, d_ref, h0r_ref, h0i_ref, br_ref, bi_ref, cr_ref, ci_ref, tab_ref,
         y_ref, hr_ref, hi_ref, xr_scr, xi_scr) = refs
    else:
        (u_ref, yin_ref, h0r_ref, h0i_ref, br_ref, bi_ref, cr_ref, ci_ref, tab_ref,
         y_ref, hr_ref, hi_ref, xr_scr, xi_scr) = refs
    tm = u_ref.shape[0]
    nblk = tm // SUBLANES
    cin = S5_WIDTH // S5_CHUNKS
    cst = S5_NSTATE // S5_CHUNKS

    @pl.when(pl.program_id(0) == 0)
    def _():
        hr_ref[...] = h0r_ref[...]
        hi_ref[...] = h0i_ref[...]

    for q in range(S5_CHUNKS):
        uq = u_ref[:, q * cin:(q + 1) * cin].astype(BF16)
        xr_scr[:, q * cst:(q + 1) * cst] = _bdot(uq, br_ref[q])
        xi_scr[:, q * cst:(q + 1) * cst] = _bdot(uq, bi_ref[q])

    edge = 0 if reverse else SUBLANES - 1
    for jc in range(S5_NSTATE // S5_LANE_CHUNK):
        sl = slice(jc * S5_LANE_CHUNK, (jc + 1) * S5_LANE_CHUNK)

        def body(r, carry, sl=sl):
            car, cai = carry
            blk = (nblk - 1 - r) if reverse else r
            row = pl.multiple_of(blk * SUBLANES, SUBLANES)
            xr = xr_scr[pl.ds(row, SUBLANES), sl]
            xi = xi_scr[pl.ds(row, SUBLANES), sl]
            for idx, dist in enumerate((1, 2, 4)):
                ar = tab_ref[2 * idx, :, sl]
                ai = tab_ref[2 * idx + 1, :, sl]
                shift = SUBLANES - dist if reverse else dist
                sr = pltpu.roll(xr, shift, 0)
                si = pltpu.roll(xi, shift, 0)
                xr, xi = xr + ar * sr - ai * si, xi + ar * si + ai * sr
            pr = tab_ref[6, :, sl]
            pi = tab_ref[7, :, sl]
            xr, xi = xr + pr * car - pi * cai, xi + pr * cai + pi * car
            xr_scr[pl.ds(row, SUBLANES), sl] = xr
            xi_scr[pl.ds(row, SUBLANES), sl] = xi
            shape = (SUBLANES, S5_LANE_CHUNK)
            return (jnp.broadcast_to(xr[edge:edge + 1, :], shape), jnp.broadcast_to(xi[edge:edge + 1, :], shape))

        car, cai = lax.fori_loop(0, nblk, body, (hr_ref[:, sl], hi_ref[:, sl]), unroll=True)
        hr_ref[:, sl] = car
        hi_ref[:, sl] = cai

    cout = S5_WIDTH // S5_CHUNKS
    for q in range(S5_CHUNKS):
        hr = xr_scr[:, q * cst:(q + 1) * cst].astype(BF16)
        hi = xi_scr[:, q * cst:(q + 1) * cst].astype(BF16)
        y = _bdot(hr, cr_ref[q]) + _bdot(hi, ci_ref[q])
        cs = slice(q * cout, (q + 1) * cout)
        if first:
            base = u_ref[:, cs] * d_ref[:, cs]
        else:
            base = yin_ref[:, cs]
        y_ref[:, cs] = base + y


def _s5_scan(u_src, extra, h0, prm, *, reverse, first):
    n = u_src.shape[0]
    tm = min(S5_TILE, n)
    nt = n // tm
    tmap = (lambda i: (nt - 1 - i, 0)) if reverse else (lambda i: (i, 0))
    full = lambda a: pl.BlockSpec(a.shape, lambda i: (0,) * a.ndim)
    extra_spec = full(extra) if first else pl.BlockSpec((tm, S5_WIDTH), tmap)
    st = jax.ShapeDtypeStruct((SUBLANES, S5_NSTATE), F32)
    st_spec = pl.BlockSpec((SUBLANES, S5_NSTATE), lambda i: (0, 0))
    return pl.pallas_call(
        functools.partial(_s5_kernel, reverse=reverse, first=first),
        grid=(nt,),
        in_specs=[pl.BlockSpec((tm, S5_WIDTH), tmap), extra_spec, st_spec, st_spec,
                  full(prm["br"]), full(prm["bi"]), full(prm["cr"]), full(prm["ci"]), full(prm["tab"])],
        out_specs=[pl.BlockSpec((tm, S5_WIDTH), tmap), st_spec, st_spec],
        out_shape=[jax.ShapeDtypeStruct((n, S5_WIDTH), F32), st, st],
        scratch_shapes=[pltpu.VMEM((tm, S5_NSTATE), F32), pltpu.VMEM((tm, S5_NSTATE), F32)],
        compiler_params=_cparams("arbitrary"),
        name="s5_scan_rev" if reverse else "s5_scan_fwd",
    )(u_src, extra, h0[0], h0[1], prm["br"], prm["bi"], prm["cr"], prm["ci"], prm["tab"])


def _s5_params(lam_re, lam_im, log_dt, b_re, b_im, c_re, c_im, reverse):
    dt = jnp.exp(log_dt)[:, None]
    mag = jnp.exp(lam_re * dt)
    a_re = mag * jnp.cos(lam_im * dt)
    a_im = mag * jnp.sin(lam_im * dt)
    den = lam_re * lam_re + lam_im * lam_im
    nr = a_re - 1.0
    k_re = (nr * lam_re + a_im * lam_im) / den
    k_im = (a_im * lam_re - nr * lam_im) / den
    bx_re = k_re[:, :, None] * b_re - k_im[:, :, None] * b_im
    bx_im = k_re[:, :, None] * b_im + k_im[:, :, None] * b_re
    gpc = S5_GROUPS // S5_CHUNKS
    eye = jnp.eye(gpc, dtype=F32)

    def bdiag_in(b):
        b = b.reshape(S5_CHUNKS, gpc, S5_STATE, S5_GROUP)
        return jnp.einsum('qgpc,gh->qgchp', b, eye).reshape(S5_CHUNKS, gpc * S5_GROUP, gpc * S5_STATE)

    def bdiag_out(c):
        c = c.reshape(S5_CHUNKS, gpc, S5_GROUP, S5_STATE)
        return jnp.einsum('qgcp,gh->qgphc', c, eye).reshape(S5_CHUNKS, gpc * S5_STATE, gpc * S5_GROUP)

    pr, pi = [a_re.reshape(-1)], [a_im.reshape(-1)]
    for _ in range(SUBLANES - 1):
        pr, pi = pr + [pr[-1] * pr[0] - pi[-1] * pi[0]], pi + [pr[-1] * pi[0] + pi[-1] * pr[0]]
    t = np.arange(SUBLANES)
    tabs = []
    for dist in (1, 2, 4):
        keep = (t <= SUBLANES - 1 - dist) if reverse else (t >= dist)
        mask = jnp.asarray(keep, F32)[:, None]
        tabs += [mask * pr[dist - 1][None, :], mask * pi[dist - 1][None, :]]
    order = [SUBLANES - 1 - i for i in range(SUBLANES)] if reverse else list(range(SUBLANES))
    tabs += [jnp.stack([pr[i] for i in order]), jnp.stack([pi[i] for i in order])]
    return {"br": bdiag_in(bx_re).astype(BF16), "bi": bdiag_in(bx_im).astype(BF16),
            "cr": bdiag_out(c_re).astype(BF16), "ci": bdiag_out(-c_im).astype(BF16),
            "tab": jnp.stack(tabs)}


def _gelu(x):
    return x * (0.5 * (1.0 + jnp.tanh(math.sqrt(2.0 / math.pi) * (x + 0.044715 * (x * x * x)))))


def _evout_kernel(x_ref, att_ref, y_ref, gt_ref, gw_ref, gb_ref, wa_ref, ws_ref, o_ref):
    g = _gelu(y_ref[...])
    z = _bdot(g.astype(BF16), gw_ref[...]) + gb_ref[...]
    ssm = g * jax.nn.sigmoid(z)
    o = _bdot(att_ref[...].astype(BF16), wa_ref[...]) + _bdot(ssm.astype(BF16), ws_ref[...])
    o_ref[...] = x_ref[...] + gt_ref[...] * o


def _even_out(x, att, y, gate, glu_w, glu_b, w_out):
    n, d = x.shape
    tm = min(512, n)
    wa = w_out[:MLA_HEADS * V_DIM].astype(BF16)
    ws = w_out[MLA_HEADS * V_DIM:].astype(BF16)
    gw = glu_w.astype(BF16)
    gb = glu_b.reshape(1, S5_WIDTH)
    full = lambda a: pl.BlockSpec(a.shape, lambda i: (0,) * a.ndim)
    return pl.pallas_call(
        _evout_kernel,
        grid=(n // tm,),
        in_specs=[pl.BlockSpec((tm, d), lambda i: (i, 0)),
                  pl.BlockSpec((tm, MLA_HEADS * V_DIM), lambda i: (i, 0)),
                  pl.BlockSpec((tm, S5_WIDTH), lambda i: (i, 0)),
                  full(gate), full(gw), full(gb), full(wa), full(ws)],
        out_specs=pl.BlockSpec((tm, d), lambda i: (i, 0)),
        out_shape=jax.ShapeDtypeStruct((n, d), F32),
        compiler_params=_cparams("parallel"),
        name="even_out",
    )(x, att, y, gate, gw, gb, wa, ws)


def _resid_kernel(x_ref, y_ref, gt_ref, w_ref, o_ref):
    o_ref[...] = x_ref[...] + gt_ref[...] * _bdot(y_ref[...].astype(BF16), w_ref[...])


def _resid_proj(x, y, gate, w):
    n, d = x.shape
    tm = min(512, n)
    wb = w.astype(BF16)
    full = lambda a: pl.BlockSpec(a.shape, lambda i: (0,) * a.ndim)
    return pl.pallas_call(
        _resid_kernel,
        grid=(n // tm,),
        in_specs=[pl.BlockSpec((tm, d), lambda i: (i, 0)), pl.BlockSpec((tm, y.shape[1]), lambda i: (i, 0)),
                  full(gate), full(wb)],
        out_specs=pl.BlockSpec((tm, d), lambda i: (i, 0)),
        out_shape=jax.ShapeDtypeStruct((n, d), F32),
        compiler_params=_cparams("parallel"),
        name="resid_proj",
    )(x, y, gate, wb)


def _sconv_kernel(z_ref, zp_ref, zn_ref, w_ref, b_ref, o_ref):
    i = pl.program_id(0)
    z = z_ref[...]
    tm = z.shape[0]
    rows = lax.broadcasted_iota(jnp.int32, z.shape, 0)
    prev_row = jnp.where(i > 0, zp_ref[SUBLANES - 1:SUBLANES, :], 0.0)
    next_row = jnp.where(i < pl.num_programs(0) - 1, zn_ref[0:1, :], 0.0)
    z_dn = jnp.where(rows == 0, prev_row, pltpu.roll(z, 1, 0))
    z_up = jnp.where(rows == tm - 1, next_row, pltpu.roll(z, tm - 1, 0))
    o_ref[...] = w_ref[0:1, :] * z_dn + w_ref[1:2, :] * z + w_ref[2:3, :] * z_up + b_ref[...]


def _short_conv(z, conv_w, conv_b):
    n, ch = z.shape
    tm = min(512, n)
    tc = 2048
    per = tm // SUBLANES
    last = n // SUBLANES - 1
    return pl.pallas_call(
        _sconv_kernel,
        grid=(n // tm, ch // tc),
        in_specs=[pl.BlockSpec((tm, tc), lambda i, j: (i, j)),
                  pl.BlockSpec((SUBLANES, tc), lambda i, j: (jnp.maximum(i * per - 1, 0), j)),
                  pl.BlockSpec((SUBLANES, tc), lambda i, j: (jnp.minimum((i + 1) * per, last), j)),
                  pl.BlockSpec((conv_w.shape[0], tc), lambda i, j: (0, j)),
                  pl.BlockSpec((1, tc), lambda i, j: (0, j))],
        out_specs=pl.BlockSpec((tm, tc), lambda i, j: (i, j)),
        out_shape=jax.ShapeDtypeStruct((n, ch), F32),
        compiler_params=_cparams("parallel", "parallel"),
        name="short_conv",
    )(z, z, z, conv_w, conv_b.reshape(1, ch))


def _filt_kernel(bands_ref, w1t_ref, w1c_ref, w1s_ref, b1_ref, w2_ref, b2_ref, w3_ref, b3_ref, fr_ref,
                 woh_ref, wol_ref, dl_ref, g_ref, s_ref, *, seq):
    i = pl.program_id(0)
    tl = g_ref.shape[0]
    rows = tl // FILT_PACK

    def lag_of(m):
        return jnp.where(m < seq, m, 2 * seq - m).astype(F32) / seq

    def packed_t(lanes_per_group):
        shape = (rows, FILT_PACK * lanes_per_group)
        r = lax.broadcasted_iota(jnp.int32, shape, 0)
        grp = lax.shift_right_logical(lax.broadcasted_iota(jnp.int32, shape, 1), int(math.log2(lanes_per_group)))
        return lag_of(i * tl + grp * rows + r)

    ang = ((2.0 * math.pi) * packed_t(HY_BANDS)) * bands_ref[...]
    fr = fr_ref[...]
    pre = (packed_t(HY_FILT_HIDDEN) * w1t_ref[...] + _hdot(jnp.cos(ang), w1c_ref[...])
           + _hdot(-jnp.sin(ang), w1s_ref[...]) + b1_ref[...])
    h = jnp.sin(fr * pre)
    h = jnp.sin(fr * (_hdot(h, w2_ref[...]) + b2_ref[...]))
    h = jnp.sin(fr * (_hdot(h, w3_ref[...]) + b3_ref[...]))
    lane = lax.broadcasted_iota(jnp.int32, (rows, LANES), 1)
    pieces = []
    for gi in range(FILT_PACK):
        slab = h[:, (gi // 2) * LANES:(gi // 2 + 1) * LANES]
        if gi % 2:
            slab = pltpu.roll(slab, HY_FILT_HIDDEN, 1)
        pieces.append(jnp.where(lane < HY_FILT_HIDDEN, slab, 0.0))
    hid = jnp.concatenate(pieces, axis=0)
    m = i * tl + lax.broadcasted_iota(jnp.int32, (tl, 1), 0)
    f = _dot3(_split(hid), (woh_ref[...], wol_ref[...])) * jnp.exp(-lag_of(m) * dl_ref[...])
    f = jnp.where(m == seq, 0.0, f)
    g_ref[...] = f

    @pl.when(i == 0)
    def _():
        s_ref[...] = jnp.zeros_like(s_ref)

    s_ref[...] += jnp.sum(jnp.abs(f), axis=0, keepdims=True)


def _hyena_filter(seq, w1, b1, w2, b2, w3, b3, freq, w_out_o):
    fh = HY_FILT_HIDDEN
    w = HY_WIDTH
    tl = min(512, seq)
    assert tl % (FILT_PACK * SUBLANES) == 0 and FILT_PACK * HY_BANDS == LANES
    eye = jnp.eye(FILT_PACK, dtype=F32)
    pad2 = lambda a: jnp.kron(eye, a)
    padv = lambda a: jnp.tile(a.reshape(1, -1), (1, FILT_PACK))
    bands = padv(jnp.linspace(1e-4, HY_BANDS - 1, HY_BANDS, dtype=F32))
    w1t = padv(w1[0])
    w1c = pad2(w1[1:1 + HY_BANDS])
    w1s = pad2(w1[1 + HY_BANDS:])
    wo_hi, wo_lo = _split(jnp.pad(w_out_o.transpose(1, 0, 2), ((0, 0), (0, LANES - fh), (0, 0))))
    deltas = jnp.abs(jnp.linspace(HY_DECAY_MIN, HY_DECAY_MAX, w, dtype=F32)).reshape(1, w)
    nt = 2 * seq // tl
    half = seq // tl
    full = lambda a: pl.BlockSpec(a.shape, lambda i: (0,) * a.ndim)
    args = (bands, w1t, w1c, w1s, padv(b1), pad2(w2), padv(b2), pad2(w3), padv(b3), padv(freq))
    return pl.pallas_call(
        functools.partial(_filt_kernel, seq=seq),
        grid=(nt,),
        in_specs=[full(a) for a in args] + [pl.BlockSpec((None, LANES, w), lambda i: (i // half, 0, 0)),
                                            pl.BlockSpec((None, LANES, w), lambda i: (i // half, 0, 0)),
                                            full(deltas)],
        out_specs=[pl.BlockSpec((tl, w), lambda i: (i, 0)), pl.BlockSpec((1, w), lambda i: (0, 0))],
        out_shape=[jax.ShapeDtypeStruct((2 * seq, w), F32), jax.ShapeDtypeStruct((1, w), F32)],
        compiler_params=_cparams("arbitrary"),
        name="hyena_filter",
    )(*args, wo_hi, wo_lo, deltas)


def _dft_consts(seq):
    n = 2 * seq
    n2 = DFT_N2
    n1 = n // n2
    k1 = n1 // 2 + 1
    two_pi = 2.0 * np.pi
    kk, nn = np.meshgrid(np.arange(k1), np.arange(n1), indexing="ij")
    ang_a = two_pi * ((kk * nn) % n1) / n1
    fa = (np.cos(ang_a), -np.sin(ang_a))
    k2, m2 = np.meshgrid(np.arange(n2), np.arange(n2), indexing="ij")
    ang_c = two_pi * ((k2 * m2) % n2) / n2
    fc = (np.cos(ang_c), -np.sin(ang_c))
    kk, mm = np.meshgrid(np.arange(k1), np.arange(n2), indexing="ij")
    ang_t = two_pi * ((kk * mm) % n) / n
    tw = (np.cos(ang_t), -np.sin(ang_t))
    weight = np.where((np.arange(k1) == 0) | (np.arange(k1) == n1 // 2), 1.0, 2.0) / n
    rows, cols = np.meshgrid(np.arange(n1 // 2), np.arange(k1), indexing="ij")
    ang_f = two_pi * ((rows * cols) % n1) / n1
    mc = np.cos(ang_f) * weight[None, :]
    ms = -np.sin(ang_f) * weight[None, :]
    kp = -(-k1 // SUBLANES) * SUBLANES
    rpad = lambda a: np.pad(a, ((0, kp - k1), (0, 0)))
    cpad = lambda a: np.pad(a, ((0, 0), (0, kp - k1)))
    f32 = lambda a: jnp.asarray(a, F32)
    fc3 = jnp.stack([_lhs3(f32(m)) for m in (fc[0], fc[1] - fc[0], fc[0] + fc[1])])
    mcs = jnp.concatenate([f32(cpad(mc)), f32(cpad(ms))], axis=1)
    return {"n1": n1, "k1": k1, "kp": kp,
            "fa_r": f32(rpad(fa[0])), "fa_i": f32(rpad(fa[1])), "fc3": fc3,
            "twa_r": f32(rpad(tw[0]).T[:, :, None]), "twa_i": f32(rpad(tw[1]).T[:, :, None]),
            "twc_r": f32(rpad(tw[0])[:, :, None]), "twc_i": f32(rpad(tw[1])[:, :, None]),
            "mcs3": _lhs3(mcs)}


def _dfta_kernel(x_ref, f_ref, twr_ref, twi_ref, or_ref, oi_ref):
    kp = or_ref.shape[0]
    f = f_ref[...]
    xt = pltpu.einshape("nsw->snw", x_ref[...])
    out_r, out_i = [], []
    for s in range(SUBLANES):
        a = _bdot(f, _rhs3(xt[s]))
        ar, ai = a[:kp], a[kp:]
        tr = twr_ref[s]
        ti = twi_ref[s]
        out_r.append(ar * tr - ai * ti)
        out_i.append(ar * ti + ai * tr)
    or_ref[...] = pltpu.einshape("skw->ksw", jnp.stack(out_r))
    oi_ref[...] = pltpu.einshape("skw->ksw", jnp.stack(out_i))


def _dft_stage_a(x3, col0, dc, n1c):
    k1 = dc["kp"]
    tw = 1024
    wt = HY_WIDTH // tw
    f3 = _lhs3(jnp.concatenate([dc["fa_r"][:, :n1c], dc["fa_i"][:, :n1c]], axis=0))
    out = jax.ShapeDtypeStruct((k1, DFT_N2, HY_WIDTH), F32)
    return pl.pallas_call(
        _dfta_kernel,
        grid=(DFT_N2 // SUBLANES, wt),
        in_specs=[pl.BlockSpec((n1c, SUBLANES, tw), lambda b, j: (0, b, col0 * wt + j)),
                  pl.BlockSpec(f3.shape, lambda b, j: (0, 0)),
                  pl.BlockSpec((SUBLANES, k1, 1), lambda b, j: (b, 0, 0)),
                  pl.BlockSpec((SUBLANES, k1, 1), lambda b, j: (b, 0, 0))],
        out_specs=[pl.BlockSpec((k1, SUBLANES, tw), lambda b, j: (0, b, j))] * 2,
        out_shape=[out, out],
        compiler_params=_cparams("parallel", "parallel"),
        name="dft_stage_a",
    )(x3, f3, dc["twa_r"], dc["twa_i"])


def _cmul3(f_ref, ar, ai):
    k1 = _bdot(f_ref[0], _rhs3(ar + ai))
    return k1 - _bdot(f_ref[2], _rhs3(ai)), k1 + _bdot(f_ref[1], _rhs3(ar))


def _dftc_filt_kernel(ar_ref, ai_ref, f_ref, s_ref, gr_ref, gi_ref, *, k1):
    @pl.when(pl.program_id(0) < k1)
    def _():
        xr, xi = _cmul3(f_ref, ar_ref[...], ai_ref[...])
        inv = 1.0 / s_ref[...]
        gr_ref[...] = xr * inv
        gi_ref[...] = xi * inv

    @pl.when(pl.program_id(0) >= k1)
    def _():
        gr_ref[...] = jnp.zeros_like(gr_ref)
        gi_ref[...] = jnp.zeros_like(gi_ref)


def _dft_filter_spectrum(ar, ai, abs_sum, dc):
    kp = dc["kp"]
    tw = 1024
    blk = pl.BlockSpec((None, DFT_N2, tw), lambda k, j: (k, 0, j))
    mat = pl.BlockSpec(dc["fc3"].shape, lambda k, j: (0, 0, 0))
    out = jax.ShapeDtypeStruct((kp, DFT_N2, HY_WIDTH), F32)
    return pl.pallas_call(
        functools.partial(_dftc_filt_kernel, k1=dc["k1"]),
        grid=(kp, HY_WIDTH // tw),
        in_specs=[blk, blk, mat, pl.BlockSpec((1, tw), lambda k, j: (0, j))],
        out_specs=[blk, blk],
        out_shape=[out, out],
        compiler_params=_cparams("parallel", "parallel"),
        name="dft_filter_spectrum",
    )(ar, ai, dc["fc3"], abs_sum)


def _dftmid_kernel(ar_ref, ai_ref, gr_ref, gi_ref, f_ref, twr_ref, twi_ref, or_ref, oi_ref, *, k1):
    @pl.when(pl.program_id(0) < k1)
    def _():
        xr, xi = _cmul3(f_ref, ar_ref[...], ai_ref[...])
        gr = gr_ref[...]
        gi = gi_ref[...]
        yr = xr * gr - xi * gi
        yi = xr * gi + xi * gr
        k1_ = _bdot(f_ref[0], _rhs3(yr + yi))
        br = k1_ + _bdot(f_ref[1], _rhs3(yi))
        bi = k1_ - _bdot(f_ref[2], _rhs3(yr))
        tr = twr_ref[...]
        ti = twi_ref[...]
        or_ref[...] = br * tr + bi * ti
        oi_ref[...] = bi * tr - br * ti

    @pl.when(pl.program_id(0) >= k1)
    def _():
        or_ref[...] = jnp.zeros_like(or_ref)
        oi_ref[...] = jnp.zeros_like(oi_ref)


def _dft_middle(ar, ai, gr, gi, dc):
    kp = dc["kp"]
    tw = 1024
    blk = pl.BlockSpec((None, DFT_N2, tw), lambda k, j: (k, 0, j))
    mat = pl.BlockSpec(dc["fc3"].shape, lambda k, j: (0, 0, 0))
    twd = pl.BlockSpec((None, DFT_N2, 1), lambda k, j: (k, 0, 0))
    out = jax.ShapeDtypeStruct((kp, DFT_N2, HY_WIDTH), F32)
    return pl.pallas_call(
        functools.partial(_dftmid_kernel, k1=dc["k1"]),
        grid=(kp, HY_WIDTH // tw),
        in_specs=[blk, blk, blk, blk, mat, twd, twd],
        out_specs=[blk, blk],
        out_shape=[out, out],
        compiler_params=_cparams("parallel", "parallel"),
        name="dft_middle",
    )(ar, ai, gr, gi, dc["fc3"], dc["twc_r"], dc["twc_i"])


def _dftf_kernel(br_ref, bi_ref, m_ref, gate_ref, y_ref, skip_ref, o_ref):
    m = m_ref[...]
    brt = pltpu.einshape("ksw->skw", br_ref[...])
    bit = pltpu.einshape("ksw->skw", bi_ref[...])
    conv = [_bdot(m, _rhs3(jnp.concatenate([brt[s], bit[s]], axis=0))) for s in range(SUBLANES)]
    conv = pltpu.einshape("snw->nsw", jnp.stack(conv))
    o_ref[...] = gate_ref[...] * (conv + skip_ref[...] * y_ref[...])


def _dft_final(br, bi, dc, gate3, gate_col, y3, y_col, skip):
    k1 = dc["kp"]
    nh = dc["n1"] // 2
    tw = 1024
    wt = HY_WIDTH // tw
    bblk = pl.BlockSpec((k1, SUBLANES, tw), lambda b, j: (0, b, j))
    mat = pl.BlockSpec(dc["mcs3"].shape, lambda b, j: (0, 0))
    return pl.pallas_call(
        _dftf_kernel,
        grid=(DFT_N2 // SUBLANES, wt),
        in_specs=[bblk, bblk, mat,
                  pl.BlockSpec((nh, SUBLANES, tw), lambda b, j: (0, b, gate_col * wt + j)),
                  pl.BlockSpec((nh, SUBLANES, tw), lambda b, j: (0, b, y_col * wt + j)),
                  pl.BlockSpec((1, tw), lambda b, j: (0, j))],
        out_specs=pl.BlockSpec((nh, SUBLANES, tw), lambda b, j: (0, b, j)),
        out_shape=jax.ShapeDtypeStruct((nh, DFT_N2, HY_WIDTH), F32),
        compiler_params=_cparams("parallel", "parallel"),
        name="dft_final",
    )(br, bi, dc["mcs3"], gate3, y3, skip)


def _hyena_mixer(x, shift, scale, gate, g_norm, w_in, conv_w, conv_b, w1, b1, w2, b2, w3, b3, freq,
                 filt_w_out, skip, w_out):
    seq = x.shape[0]
    dc = _dft_consts(seq)
    nh = dc["n1"] // 2
    z = _norm_proj(x, shift, scale, g_norm, w_in)
    zc = _short_conv(z, conv_w, conv_b)
    zc3 = zc.reshape(nh, DFT_N2, 3 * HY_WIDTH)
    y3, y_col = zc3, 2
    for o in range(2):
        g, abs_sum = _hyena_filter(seq, w1, b1, w2, b2, w3, b3, freq, filt_w_out[:, o])
        ga_r, ga_i = _dft_stage_a(g.reshape(dc["n1"], DFT_N2, HY_WIDTH), 0, dc, dc["n1"])
        gr, gi = _dft_filter_spectrum(ga_r, ga_i, abs_sum, dc)
        ya_r, ya_i = _dft_stage_a(y3, y_col, dc, nh)
        br, bi = _dft_middle(ya_r, ya_i, gr, gi, dc)
        y3 = _dft_final(br, bi, dc, zc3, o, y3, y_col, skip[o].reshape(1, HY_WIDTH))
        y_col = 0
    return _resid_proj(x, y3.reshape(seq, HY_WIDTH), gate, w_out)


def _even_mixer(x, cx, mx, mc, g_norm, ev_w_in, q_a_norm_g, w_uq, kv_a_norm_g, w_ukv, q_head_g, k_head_g,
                lam_re, lam_im, log_dt, b_re, b_im, c_re, c_im, d_skip, glu_w, glu_b, ev_w_out):
    n = x.shape[0]
    nc = cx.shape[0]
    cuts = (Q_LORA, Q_LORA + KV_LORA, Q_LORA + KV_LORA + QK_ROPE)
    w_qa, w_kva, w_kpe, w_u = (ev_w_in[:, :cuts[0]], ev_w_in[:, cuts[0]:cuts[1]],
                               ev_w_in[:, cuts[1]:cuts[2]], ev_w_in[:, cuts[2]:])
    w_all = jnp.concatenate([w_u, w_qa, w_kva, _pad_lanes(w_kpe), _pad_lanes(w_kpe[:, _ROPE_SWAP])], axis=1)
    proj_x = _norm_proj(x, mx[3], mx[4], g_norm, w_all)
    proj_c = _norm_proj(cx, mc[3], mc[4], g_norm, w_all)

    wq3, wkv, gq3, gk3 = _mla_weights(w_uq, w_ukv, q_head_g, k_head_g)
    gqa = q_a_norm_g.reshape(1, Q_LORA)
    gkva = kv_a_norm_g.reshape(1, KV_LORA)
    cos_x, sin_x = _rope_tables(n)
    cos_c = _pad_lanes(jnp.ones((nc, QK_ROPE), F32))
    sin_c = jnp.zeros((nc, LANES), F32)
    q_x, k_x, v_x = _mla_heads(proj_x, cos_x, sin_x, gqa, gkva, wq3, wkv, gq3, gk3)
    _, k_c, v_c = _mla_heads(proj_c, cos_c, sin_c, gqa, gkva, wq3, wkv, gq3, gk3)
    att = _attention(q_x, jnp.concatenate([k_x, k_c], axis=1), jnp.concatenate([v_x, v_c], axis=1))

    d_row = d_skip.reshape(1, S5_WIDTH)
    zero = jnp.zeros((SUBLANES, S5_NSTATE), F32)
    y = None
    for direction, reverse in ((0, False), (1, True)):
        prm = _s5_params(lam_re[direction], lam_im[direction], log_dt[direction], b_re[direction],
                         b_im[direction], c_re[direction], c_im[direction], reverse)
        _, hc_r, hc_i = _s5_scan(proj_c, d_row, (zero, zero), prm, reverse=reverse, first=True)
        if y is None:
            y, _, _ = _s5_scan(proj_x, d_row, (hc_r, hc_i), prm, reverse=reverse, first=True)
        else:
            y, _, _ = _s5_scan(proj_x, y, (hc_r, hc_i), prm, reverse=reverse, first=False)
    return _even_out(x, att, y, mx[5], glu_w, glu_b, ev_w_out)


def kernel(x, c, ctx, c_ctx, ada_w, ada_b, norm_g, ffn_w_in, ffn_w_out, ev_w_in, mla_q_a_norm_g, mla_w_uq, mla_kv_a_norm_g, mla_w_ukv, mla_q_head_g, mla_k_head_g, s5_lam_re, s5_lam_im, s5_log_dt, s5_b_re, s5_b_im, s5_c_re, s5_c_im, s5_d, s5_glu_w, s5_glu_b, ev_w_out, hy_w_in, hy_conv_w, hy_conv_b, hy_filt_w1, hy_filt_b1, hy_filt_w2, hy_filt_b2, hy_filt_w3, hy_filt_b3, hy_filt_freq, hy_filt_w_out, hy_skip, hy_w_out):
    assert x.shape[0] == 1 and ctx.shape[0] == 1, "kernels are written for batch 1"
    depth = ada_w.shape[0]
    mods = _ada_mods(c, c_ctx, ada_w, ada_b)
    xs = x[0]
    cx = ctx[0]
    for i in range(depth):
        even = i % 2 == 0
        li = i // 2
        need_after = any(j % 2 == 0 for j in range(i + 1, depth))
        use_ctx = even or need_after
        assert not need_after, "context outputs after an even layer are not needed at this depth"
        mx = mods[i, 0]
        mc = mods[i, 1]
        xs = _ffn_half(xs, mx, 0, norm_g[i, 0], ffn_w_in[i, 0], ffn_w_out[i, 0])
        if use_ctx:
            cx = _ffn_half(cx, mc, 0, norm_g[i, 0], ffn_w_in[i, 0], ffn_w_out[i, 0])
        if even:
            xs = _even_mixer(xs, cx, mx, mc, norm_g[i, 1], ev_w_in[li], mla_q_a_norm_g[li], mla_w_uq[li],
                             mla_kv_a_norm_g[li], mla_w_ukv[li], mla_q_head_g[li], mla_k_head_g[li],
                             s5_lam_re[li], s5_lam_im[li], s5_log_dt[li], s5_b_re[li], s5_b_im[li],
                             s5_c_re[li], s5_c_im[li], s5_d[li], s5_glu_w[li], s5_glu_b[li], ev_w_out[li])
        else:
            xs = _hyena_mixer(xs, mx[3], mx[4], mx[5], norm_g[i, 1], hy_w_in[li], hy_conv_w[li], hy_conv_b[li],
                              hy_filt_w1[li], hy_filt_b1[li], hy_filt_w2[li], hy_filt_b2[li], hy_filt_w3[li],
                              hy_filt_b3[li], hy_filt_freq[li], hy_filt_w_out[li], hy_skip[li], hy_w_out[li])
        xs = _ffn_half(xs, mx, 6, norm_g[i, 2], ffn_w_in[i, 1], ffn_w_out[i, 1])
    return xs[None]
```

```python
import functools
import math

import numpy as np
import jax
import jax.numpy as jnp
from jax import lax
from jax.experimental import pallas as pl
from jax.experimental.pallas import tpu as pltpu

F32 = jnp.float32
BF16 = jnp.bfloat16
HIGHEST = lax.Precision.HIGHEST

D_MODEL = 2048
N_MOD = 9
D_FF = 5632
NORM_EPS = 1e-6
GRID_W = 64
MLA_HEADS = 8
QK_NOPE = 128
QK_ROPE = 64
QK_DIM = QK_NOPE + QK_ROPE
V_DIM = 128
Q_LORA = 512
KV_LORA = 256
ROPE_BASE = 10000.0
S5_WIDTH = 1024
S5_GROUP = 16
S5_GROUPS = S5_WIDTH // S5_GROUP
S5_STATE = 64
S5_NSTATE = S5_GROUPS * S5_STATE
S5_CHUNKS = 4
HY_WIDTH = D_MODEL
HY_BANDS = 16
HY_FILT_HIDDEN = 64
HY_DECAY_MIN = math.log(1e-2) / 1.5
HY_DECAY_MAX = math.log(1e-2) / 0.3
DFT_N2 = 256
FILT_PACK = 8
LANES = 128
SUBLANES = 8
VMEM_LIMIT = 56 * 1024 * 1024


def _cparams(*sem):
    return pltpu.CompilerParams(dimension_semantics=sem, vmem_limit_bytes=VMEM_LIMIT)


def _hdot(a, b):
    return jnp.dot(a, b, preferred_element_type=F32, precision=HIGHEST)


def _bdot(a, b):
    return jnp.dot(a, b, preferred_element_type=F32)


def _split(x):
    hi = x.astype(BF16)
    return hi, (x - hi.astype(F32)).astype(BF16)


def _dot3(a, b):
    return _bdot(a[0], b[0]) + (_bdot(a[1], b[0]) + _bdot(a[0], b[1]))


def _lhs3(a):
    hi, lo = _split(a)
    return jnp.concatenate([hi, lo, hi], axis=1)


def _rhs3(b):
    hi, lo = _split(b)
    return jnp.concatenate([hi, hi, lo], axis=0)


def _norm_mod(x, g, scale, shift):
    y = x * lax.rsqrt(jnp.mean(x * x, axis=-1, keepdims=True) + NORM_EPS) * g
    return y * (1.0 + scale) + shift


def _mods_kernel(s_ref, w_ref, b_ref, o_ref):
    s = s_ref[...]
    s = s * jax.nn.sigmoid(s)
    o_ref[...] = _hdot(s, w_ref[...]) + b_ref[...]


def _ada_mods(c, c_ctx, ada_w, ada_b):
    depth, d, nd = ada_w.shape
    s = jnp.zeros((SUBLANES, d), F32).at[0].set(c[0]).at[1].set(c_ctx)
    tn = 1024
    out = pl.pallas_call(
        _mods_kernel,
        grid=(depth, nd // tn),
        in_specs=[pl.BlockSpec((SUBLANES, d), lambda l, j: (0, 0)),
                  pl.BlockSpec((None, d, tn), lambda l, j: (l, 0, j)),
                  pl.BlockSpec((None, 1, tn), lambda l, j: (l, 0, j))],
        out_specs=pl.BlockSpec((None, SUBLANES, tn), lambda l, j: (l, 0, j)),
        out_shape=jax.ShapeDtypeStruct((depth, SUBLANES, nd), F32),
        compiler_params=_cparams("parallel", "parallel"),
        name="ada_mods",
    )(s, ada_w, ada_b.reshape(depth, 1, nd))
    return out[:, :2].reshape(depth, 2, N_MOD, 1, d)


FFN_NORM_SLICES = 8


def _ffn_kernel(x_ref, xn_ref, sh_ref, sc_ref, gt_ref, g_ref, wg_ref, wu_ref, wo_ref, o_ref,
                h0_scr, h1_scr, acc_scr):
    i = pl.program_id(0)
    j = pl.program_id(1)
    rows = x_ref.shape[0] // FFN_NORM_SLICES

    def norm(x):
        return _norm_mod(x, g_ref[...], sc_ref[...], sh_ref[...]).astype(BF16)

    @pl.when((i == 0) & (j == 0))
    def _():
        h0_scr[...] = norm(x_ref[...])

    @pl.when(j == 0)
    def _():
        acc_scr[...] = jnp.zeros_like(acc_scr)

    def step(h_cur, h_nxt):
        r0 = pl.multiple_of(jnp.minimum(j, FFN_NORM_SLICES - 1) * rows, rows)
        h_nxt[pl.ds(r0, rows), :] = norm(xn_ref[pl.ds(r0, rows), :])
        h = h_cur[...]
        g = _bdot(h, wg_ref[...])
        u = _bdot(h, wu_ref[...])
        a = (g * jax.nn.sigmoid(g)) * u
        acc_scr[...] += _bdot(a.astype(BF16), wo_ref[...])

    @pl.when(lax.rem(i, 2) == 0)
    def _():
        step(h0_scr, h1_scr)

    @pl.when(lax.rem(i, 2) == 1)
    def _():
        step(h1_scr, h0_scr)

    @pl.when(j == pl.num_programs(1) - 1)
    def _():
        o_ref[...] = x_ref[...] + gt_ref[...] * (0.5 * acc_scr[...])


def _ffn_half(x, mod, k0, g_norm, w_in, w_out):
    n, d = x.shape
    f = w_out.shape[0]
    tm = min(512, n)
    tf = 512
    nf = f // tf
    nt = n // tm
    assert nf >= FFN_NORM_SLICES and tm % (FFN_NORM_SLICES * 2 * SUBLANES) == 0
    vec = pl.BlockSpec((1, d), lambda i, j: (0, 0))
    return pl.pallas_call(
        _ffn_kernel,
        grid=(nt, nf),
        in_specs=[pl.BlockSpec((tm, d), lambda i, j: (i, 0)),
                  pl.BlockSpec((tm, d), lambda i, j: (jnp.minimum(i + 1, nt - 1), 0)),
                  vec, vec, vec, vec,
                  pl.BlockSpec((d, tf), lambda i, j: (0, j)),
                  pl.BlockSpec((d, tf), lambda i, j: (0, j + nf)),
                  pl.BlockSpec((tf, d), lambda i, j: (j, 0))],
        out_specs=pl.BlockSpec((tm, d), lambda i, j: (i, 0)),
        out_shape=jax.ShapeDtypeStruct((n, d), F32),
        scratch_shapes=[pltpu.VMEM((tm, d), BF16), pltpu.VMEM((tm, d), BF16), pltpu.VMEM((tm, d), F32)],
        compiler_params=_cparams("arbitrary", "arbitrary"),
        name="ffn_half",
    )(x, x, mod[k0], mod[k0 + 1], mod[k0 + 2], g_norm.reshape(1, d),
      w_in.astype(BF16), w_in.astype(BF16), w_out.astype(BF16))


def _proj_kernel(x_ref, sh_ref, sc_ref, g_ref, w_ref, o_ref, h_scr):
    @pl.when(pl.program_id(1) == 0)
    def _():
        h_scr[...] = _norm_mod(x_ref[...], g_ref[...], sc_ref[...], sh_ref[...]).astype(BF16)

    o_ref[...] = _bdot(h_scr[...], w_ref[...])


def _norm_proj(x, shift, scale, g_norm, w):
    n, d = x.shape
    nout = w.shape[1]
    tm = min(512, n)
    tn = 2048
    vec = pl.BlockSpec((1, d), lambda i, j: (0, 0))
    return pl.pallas_call(
        _proj_kernel,
        grid=(n // tm, nout // tn),
        in_specs=[pl.BlockSpec((tm, d), lambda i, j: (i, 0)), vec, vec, vec,
                  pl.BlockSpec((d, tn), lambda i, j: (0, j))],
        out_specs=pl.BlockSpec((tm, tn), lambda i, j: (i, j)),
        out_shape=jax.ShapeDtypeStruct((n, nout), F32),
        scratch_shapes=[pltpu.VMEM((tm, d), BF16)],
        compiler_params=_cparams("parallel", "arbitrary"),
        name="norm_proj",
    )(x, shift, scale, g_norm.reshape(1, d), w.astype(BF16))


def _mla_kernel(qa_ref, kva_ref, kpe_ref, cos_ref, sin_ref, gqa_ref, gkva_ref, wq_ref, wkv_ref,
                gq_ref, gk_ref, q_ref, k_ref, v_ref):
    def rms(t, g):
        return t * lax.rsqrt(jnp.mean(t * t, axis=-1, keepdims=True) + NORM_EPS) * g

    qn = rms(qa_ref[...], gqa_ref[...]).astype(BF16)
    kvn = rms(kva_ref[...], gkva_ref[...]).astype(BF16)
    cos = cos_ref[...]
    sin = sin_ref[...]
    gq = gq_ref[...]
    gk = gk_ref[...]
    kpe = kpe_ref[:, :LANES]
    kpe_sw = kpe_ref[:, LANES:]
    kpe_ss = jnp.sum(kpe * kpe, axis=-1, keepdims=True)
    k_rot = (kpe * gk[:, LANES:2 * LANES]) * cos + (kpe_sw * gk[:, 2 * LANES:]) * sin
    inv_dim = 1.0 / QK_DIM
    q_scale = math.log2(math.e) / math.sqrt(QK_DIM)
    for h in range(MLA_HEADS):
        qh = _bdot(qn, wq_ref[h])
        nope, pe, pe_sw = qh[:, :LANES], qh[:, LANES:2 * LANES], qh[:, 2 * LANES:]
        ss = jnp.sum(nope * nope, axis=-1, keepdims=True) + jnp.sum(pe * pe, axis=-1, keepdims=True)
        r = lax.rsqrt(ss * inv_dim + NORM_EPS) * q_scale
        q_rot = (pe * gq[:, LANES:2 * LANES]) * cos + (pe_sw * gq[:, 2 * LANES:]) * sin
        q_ref[h, :, :LANES] = (nope * r * gq[:, :LANES]).astype(BF16)
        q_ref[h, :, LANES:] = (q_rot * r).astype(BF16)
        kv = _bdot(kvn, wkv_ref[h])
        k_nope, vv = kv[:, :LANES], kv[:, LANES:]
        ssk = jnp.sum(k_nope * k_nope, axis=-1, keepdims=True) + kpe_ss
        rk = lax.rsqrt(ssk * inv_dim + NORM_EPS)
        k_ref[h, :, :LANES] = (k_nope * rk * gk[:, :LANES]).astype(BF16)
        k_ref[h, :, LANES:] = (k_rot * rk).astype(BF16)
        v_ref[h] = vv.astype(BF16)


_ROPE_SWAP = np.concatenate([np.arange(16, 32), np.arange(0, 16), np.arange(48, 64), np.arange(32, 48)])


def _pad_lanes(a, width=LANES):
    return jnp.pad(a, [(0, 0)] * (a.ndim - 1) + [(0, width - a.shape[-1])])


def _mla_weights(w_uq, w_ukv, q_head_g, k_head_g):
    wq = w_uq.reshape(Q_LORA, MLA_HEADS, QK_DIM).transpose(1, 0, 2)
    wq_pe = wq[:, :, QK_NOPE:]
    wq3 = jnp.concatenate([wq[:, :, :QK_NOPE], _pad_lanes(wq_pe), _pad_lanes(wq_pe[:, :, _ROPE_SWAP])], axis=-1)
    wkv = w_ukv.reshape(KV_LORA, MLA_HEADS, QK_NOPE + V_DIM).transpose(1, 0, 2)

    def g3(g):
        pe = g[QK_NOPE:]
        return jnp.concatenate([g[:QK_NOPE], _pad_lanes(pe), _pad_lanes(pe[_ROPE_SWAP])]).reshape(1, 3 * LANES)

    return wq3.astype(BF16), wkv.astype(BF16), g3(q_head_g), g3(k_head_g)


def _rope_tables(n):
    row = jnp.repeat(jnp.arange(n // GRID_W, dtype=F32), GRID_W)
    col = jnp.tile(jnp.arange(GRID_W, dtype=F32), n // GRID_W)
    n_freq = QK_ROPE // 4
    inv_freq = ROPE_BASE ** (-jnp.arange(n_freq, dtype=F32) / n_freq)
    ang_r = row[:, None] * inv_freq
    ang_c = col[:, None] * inv_freq
    cr, sr, cc, sc = jnp.cos(ang_r), jnp.sin(ang_r), jnp.cos(ang_c), jnp.sin(ang_c)
    cos = jnp.concatenate([cr, cr, cc, cc], axis=-1)
    sin = jnp.concatenate([-sr, sr, -sc, sc], axis=-1)
    return _pad_lanes(cos), _pad_lanes(sin)


def _mla_heads(proj, cos, sin, gqa, gkva, wq3, wkv, gq3, gk3):
    n = proj.shape[0]
    tm = min(512, n)
    hd = 2 * LANES
    full = lambda a: pl.BlockSpec(a.shape, lambda i: (0,) * a.ndim)
    return pl.pallas_call(
        _mla_kernel,
        grid=(n // tm,),
        in_specs=[pl.BlockSpec((tm, Q_LORA), lambda i: (i, S5_WIDTH // Q_LORA)),
                  pl.BlockSpec((tm, KV_LORA), lambda i: (i, (S5_WIDTH + Q_LORA) // KV_LORA)),
                  pl.BlockSpec((tm, 2 * LANES), lambda i: (i, (S5_WIDTH + Q_LORA + KV_LORA) // (2 * LANES))),
                  pl.BlockSpec((tm, LANES), lambda i: (i, 0)),
                  pl.BlockSpec((tm, LANES), lambda i: (i, 0)),
                  full(gqa), full(gkva), full(wq3), full(wkv), full(gq3), full(gk3)],
        out_specs=[pl.BlockSpec((MLA_HEADS, tm, hd), lambda i: (0, i, 0)),
                   pl.BlockSpec((MLA_HEADS, tm, hd), lambda i: (0, i, 0)),
                   pl.BlockSpec((MLA_HEADS, tm, V_DIM), lambda i: (0, i, 0))],
        out_shape=[jax.ShapeDtypeStruct((MLA_HEADS, n, hd), BF16),
                   jax.ShapeDtypeStruct((MLA_HEADS, n, hd), BF16),
                   jax.ShapeDtypeStruct((MLA_HEADS, n, V_DIM), BF16)],
        compiler_params=_cparams("parallel"),
        name="mla_heads",
    )(proj, proj, proj, cos, sin, gqa, gkva, wq3, wkv, gq3, gk3)


def _attn_kernel(q_ref, k_ref, v_ref, o_ref, s_scr, bm_scr, m_scr, acc_scr, *, nkv):
    t = pl.program_id(0)
    tk = k_ref.shape[0]
    jp = lax.rem(t + (nkv - 1), nkv)

    @pl.when(t == 0)
    def _():
        s_scr[1] = jnp.zeros(s_scr.shape[1:], F32)
        bm_scr[1] = jnp.zeros(bm_scr.shape[1:], F32)
        m_scr[...] = jnp.zeros_like(m_scr)
        acc_scr[...] = jnp.zeros_like(acc_scr)

    def step(cur, prev):
        s_new = lax.dot_general(q_ref[...], k_ref[...], (((1,), (1,)), ((), ())), preferred_element_type=F32)
        s_scr[cur] = s_new
        bm_scr[cur] = jnp.max(s_new, axis=-1, keepdims=True)
        m_prev = jnp.where(jp == 0, -jnp.inf, m_scr[...])
        m_new = jnp.maximum(m_prev, bm_scr[prev])
        alpha = jnp.exp2(m_prev - m_new)
        p = jnp.exp2(s_scr[prev] - m_new).astype(BF16)
        v_ext = jnp.concatenate([v_ref[...], jnp.ones((tk, LANES), BF16)], axis=1)
        acc_scr[...] = alpha * acc_scr[...] + _bdot(p, v_ext)
        m_scr[...] = m_new

    @pl.when(lax.rem(t, 2) == 0)
    def _():
        step(0, 1)

    @pl.when(lax.rem(t, 2) == 1)
    def _():
        step(1, 0)

    @pl.when((jp == nkv - 1) & (t > 0))
    def _():
        acc = acc_scr[...]
        o_ref[...] = acc[:, :V_DIM] / acc[:, V_DIM:]


def _kv_tile(m):
    for t in (3328, 1280, 1024, 768, 512, 256, 128):
        if m % t == 0:
            return t
    raise ValueError(f"key length {m} is not a multiple of {LANES}")


def _attention(q, k, v):
    h, n, hd = q.shape
    m = k.shape[1]
    tq = min(1024, n)
    tk = _kv_tile(m)
    nq = n // tq
    nkv = m // tk
    total = h * nq * nkv

    def qk_idx(t):
        tt = jnp.minimum(t, total - 1)
        return tt // (nq * nkv), (tt // nkv) % nq, tt % nkv

    def pv_idx(t):
        tt = jnp.maximum(t - 1, 0)
        return tt // (nq * nkv), (tt // nkv) % nq, tt % nkv

    return pl.pallas_call(
        functools.partial(_attn_kernel, nkv=nkv),
        grid=(total + 1,),
        in_specs=[pl.BlockSpec((None, tq, hd), lambda t: (qk_idx(t)[0], qk_idx(t)[1], 0)),
                  pl.BlockSpec((None, tk, hd), lambda t: (qk_idx(t)[0], qk_idx(t)[2], 0)),
                  pl.BlockSpec((None, tk, V_DIM), lambda t: (pv_idx(t)[0], pv_idx(t)[2], 0))],
        out_specs=pl.BlockSpec((tq, V_DIM), lambda t: (pv_idx(t)[1], pv_idx(t)[0])),
        out_shape=jax.ShapeDtypeStruct((n, h * V_DIM), F32),
        scratch_shapes=[pltpu.VMEM((2, tq, tk), F32), pltpu.VMEM((2, tq, 1), F32),
                        pltpu.VMEM((tq, 1), F32), pltpu.VMEM((tq, V_DIM + LANES), F32)],
        compiler_params=_cparams("arbitrary"),
        name="flash_attention",
    )(q, k, v)


S5_TILE = 256
S5_LANE_CHUNK = 512


def _s5_kernel(*refs, reverse, first):
    if first:
        (u_ref, d_ref, h0r_ref, h0i_ref, br_ref, bi_ref, cr_ref, ci_ref, tab_ref,
         y_ref, hr_ref, hi_ref, xr_scr, xi_scr) = refs
    else:
        (u_ref, yin_ref, h0r_ref, h0i_ref, br_ref, bi_ref, cr_ref, ci_ref, tab_ref,
         y_ref, hr_ref, hi_ref, xr_scr, xi_scr) = refs
    tm = u_ref.shape[0]
    nblk = tm // SUBLANES
    cin = S5_WIDTH // S5_CHUNKS
    cst = S5_NSTATE // S5_CHUNKS

    @pl.when(pl.program_id(0) == 0)
    def _():
        hr_ref[...] = h0r_ref[...]
        hi_ref[...] = h0i_ref[...]

    for q in range(S5_CHUNKS):
        uq = u_ref[:, q * cin:(q + 1) * cin].astype(BF16)
        xr_scr[:, q * cst:(q + 1) * cst] = _bdot(uq, br_ref[q])
        xi_scr[:, q * cst:(q + 1) * cst] = _bdot(uq, bi_ref[q])

    edge = 0 if reverse else SUBLANES - 1
    for jc in range(S5_NSTATE // S5_LANE_CHUNK):
        sl = slice(jc * S5_LANE_CHUNK, (jc + 1) * S5_LANE_CHUNK)

        def body(r, carry, sl=sl):
            car, cai = carry
            blk = (nblk - 1 - r) if reverse else r
            row = pl.multiple_of(blk * SUBLANES, SUBLANES)
            xr = xr_scr[pl.ds(row, SUBLANES), sl]
            xi = xi_scr[pl.ds(row, SUBLANES), sl]
            for idx, dist in enumerate((1, 2, 4)):
                ar = tab_ref[2 * idx, :, sl]
                ai = tab_ref[2 * idx + 1, :, sl]
                shift = SUBLANES - dist if reverse else dist
                sr = pltpu.roll(xr, shift, 0)
                si = pltpu.roll(xi, shift, 0)
                xr, xi = xr + ar * sr - ai * si, xi + ar * si + ai * sr
            pr = tab_ref[6, :, sl]
            pi = tab_ref[7, :, sl]
            xr, xi = xr + pr * car - pi * cai, xi + pr * cai + pi * car
            xr_scr[pl.ds(row, SUBLANES), sl] = xr
            xi_scr[pl.ds(row, SUBLANES), sl] = xi
            shape = (SUBLANES, S5_LANE_CHUNK)
            return (jnp.broadcast_to(xr[edge:edge + 1, :], shape), jnp.broadcast_to(xi[edge:edge + 1, :], shape))

        car, cai = lax.fori_loop(0, nblk, body, (hr_ref[:, sl], hi_ref[:, sl]), unroll=True)
        hr_ref[:, sl] = car
        hi_ref[:, sl] = cai

    cout = S5_WIDTH // S5_CHUNKS
    for q in range(S5_CHUNKS):
        hr = xr_scr[:, q * cst:(q + 1) * cst].astype(BF16)
        hi = xi_scr[:, q * cst:(q + 1) * cst].astype(BF16)
        y = _bdot(hr, cr_ref[q]) + _bdot(hi, ci_ref[q])
        cs = slice(q * cout, (q + 1) * cout)
        if first:
            base = u_ref[:, cs] * d_ref[:, cs]
        else:
            base = yin_ref[:, cs]
        y_ref[:, cs] = base + y


def _s5_scan(u_src, extra, h0, prm, *, reverse, first):
    n = u_src.shape[0]
    tm = min(S5_TILE, n)
    nt = n // tm
    tmap = (lambda i: (nt - 1 - i, 0)) if reverse else (lambda i: (i, 0))
    full = lambda a: pl.BlockSpec(a.shape, lambda i: (0,) * a.ndim)
    extra_spec = full(extra) if first else pl.BlockSpec((tm, S5_WIDTH), tmap)
    st = jax.ShapeDtypeStruct((SUBLANES, S5_NSTATE), F32)
    st_spec = pl.BlockSpec((SUBLANES, S5_NSTATE), lambda i: (0, 0))
    return pl.pallas_call(
        functools.partial(_s5_kernel, reverse=reverse, first=first),
        grid=(nt,),
        in_specs=[pl.BlockSpec((tm, S5_WIDTH), tmap), extra_spec, st_spec, st_spec,
                  full(prm["br"]), full(prm["bi"]), full(prm["cr"]), full(prm["ci"]), full(prm["tab"])],
        out_specs=[pl.BlockSpec((tm, S5_WIDTH), tmap), st_spec, st_spec],
        out_shape=[jax.ShapeDtypeStruct((n, S5_WIDTH), F32), st, st],
        scratch_shapes=[pltpu.VMEM((tm, S5_NSTATE), F32), pltpu.VMEM((tm, S5_NSTATE), F32)],
        compiler_params=_cparams("arbitrary"),
        name="s5_scan_rev" if reverse else "s5_scan_fwd",
    )(u_src, extra, h0[0], h0[1], prm["br"], prm["bi"], prm["cr"], prm["ci"], prm["tab"])


def _s5_params(lam_re, lam_im, log_dt, b_re, b_im, c_re, c_im, reverse):
    dt = jnp.exp(log_dt)[:, None]
    mag = jnp.exp(lam_re * dt)
    a_re = mag * jnp.cos(lam_im * dt)
    a_im = mag * jnp.sin(lam_im * dt)
    den = lam_re * lam_re + lam_im * lam_im
    nr = a_re - 1.0
    k_re = (nr * lam_re + a_im * lam_im) / den
    k_im = (a_im * lam_re - nr * lam_im) / den
    bx_re = k_re[:, :, None] * b_re - k_im[:, :, None] * b_im
    bx_im = k_re[:, :, None] * b_im + k_im[:, :, None] * b_re
    gpc = S5_GROUPS // S5_CHUNKS
    eye = jnp.eye(gpc, dtype=F32)

    def bdiag_in(b):
        b = b.reshape(S5_CHUNKS, gpc, S5_STATE, S5_GROUP)
        return jnp.einsum('qgpc,gh->qgchp', b, eye).reshape(S5_CHUNKS, gpc * S5_GROUP, gpc * S5_STATE)

    def bdiag_out(c):
        c = c.reshape(S5_CHUNKS, gpc, S5_GROUP, S5_STATE)
        return jnp.einsum('qgcp,gh->qgphc', c, eye).reshape(S5_CHUNKS, gpc * S5_STATE, gpc * S5_GROUP)

    pr, pi = [a_re.reshape(-1)], [a_im.reshape(-1)]
    for _ in range(SUBLANES - 1):
        pr, pi = pr + [pr[-1] * pr[0] - pi[-1] * pi[0]], pi + [pr[-1] * pi[0] + pi[-1] * pr[0]]
    t = np.arange(SUBLANES)
    tabs = []
    for dist in (1, 2, 4):
        keep = (t <= SUBLANES - 1 - dist) if reverse else (t >= dist)
        mask = jnp.asarray(keep, F32)[:, None]
        tabs += [mask * pr[dist - 1][None, :], mask * pi[dist - 1][None, :]]
    order = [SUBLANES - 1 - i for i in range(SUBLANES)] if reverse else list(range(SUBLANES))
    tabs += [jnp.stack([pr[i] for i in order]), jnp.stack([pi[i] for i in order])]
    return {"br": bdiag_in(bx_re).astype(BF16), "bi": bdiag_in(bx_im).astype(BF16),
            "cr": bdiag_out(c_re).astype(BF16), "ci": bdiag_out(-c_im).astype(BF16),
            "tab": jnp.stack(tabs)}


def _gelu(x):
    return x * (0.5 * (1.0 + jnp.tanh(math.sqrt(2.0 / math.pi) * (x + 0.044715 * (x * x * x)))))


def _evout_kernel(x_ref, att_ref, y_ref, gt_ref, gw_ref, gb_ref, wa_ref, ws_ref, o_ref):
    g = _gelu(y_ref[...])
    z = _bdot(g.astype(BF16), gw_ref[...]) + gb_ref[...]
    ssm = g * jax.nn.sigmoid(z)
    o = _bdot(att_ref[...].astype(BF16), wa_ref[...]) + _bdot(ssm.astype(BF16), ws_ref[...])
    o_ref[...] = x_ref[...] + gt_ref[...] * o


def _even_out(x, att, y, gate, glu_w, glu_b, w_out):
    n, d = x.shape
    tm = min(512, n)
    wa = w_out[:MLA_HEADS * V_DIM].astype(BF16)
    ws = w_out[MLA_HEADS * V_DIM:].astype(BF16)
    gw = glu_w.astype(BF16)
    gb = glu_b.reshape(1, S5_WIDTH)
    full = lambda a: pl.BlockSpec(a.shape, lambda i: (0,) * a.ndim)
    return pl.pallas_call(
        _evout_kernel,
        grid=(n // tm,),
        in_specs=[pl.BlockSpec((tm, d), lambda i: (i, 0)),
                  pl.BlockSpec((tm, MLA_HEADS * V_DIM), lambda i: (i, 0)),
                  pl.BlockSpec((tm, S5_WIDTH), lambda i: (i, 0)),
                  full(gate), full(gw), full(gb), full(wa), full(ws)],
        out_specs=pl.BlockSpec((tm, d), lambda i: (i, 0)),
        out_shape=jax.ShapeDtypeStruct((n, d), F32),
        compiler_params=_cparams("parallel"),
        name="even_out",
    )(x, att, y, gate, gw, gb, wa, ws)


def _resid_kernel(x_ref, y_ref, gt_ref, w_ref, o_ref):
    o_ref[...] = x_ref[...] + gt_ref[...] * _bdot(y_ref[...].astype(BF16), w_ref[...])


def _resid_proj(x, y, gate, w):
    n, d = x.shape
    tm = min(512, n)
    wb = w.astype(BF16)
    full = lambda a: pl.BlockSpec(a.shape, lambda i: (0,) * a.ndim)
    return pl.pallas_call(
        _resid_kernel,
        grid=(n // tm,),
        in_specs=[pl.BlockSpec((tm, d), lambda i: (i, 0)), pl.BlockSpec((tm, y.shape[1]), lambda i: (i, 0)),
                  full(gate), full(wb)],
        out_specs=pl.BlockSpec((tm, d), lambda i: (i, 0)),
        out_shape=jax.ShapeDtypeStruct((n, d), F32),
        compiler_params=_cparams("parallel"),
        name="resid_proj",
    )(x, y, gate, wb)


def _sconv_kernel(z_ref, zp_ref, zn_ref, w_ref, b_ref, o_ref):
    i = pl.program_id(0)
    z = z_ref[...]
    tm = z.shape[0]
    rows = lax.broadcasted_iota(jnp.int32, z.shape, 0)
    prev_row = jnp.where(i > 0, zp_ref[SUBLANES - 1:SUBLANES, :], 0.0)
    next_row = jnp.where(i < pl.num_programs(0) - 1, zn_ref[0:1, :], 0.0)
    z_dn = jnp.where(rows == 0, prev_row, pltpu.roll(z, 1, 0))
    z_up = jnp.where(rows == tm - 1, next_row, pltpu.roll(z, tm - 1, 0))
    o_ref[...] = w_ref[0:1, :] * z_dn + w_ref[1:2, :] * z + w_ref[2:3, :] * z_up + b_ref[...]


def _short_conv(z, conv_w, conv_b):
    n, ch = z.shape
    tm = min(512, n)
    tc = 2048
    per = tm // SUBLANES
    last = n // SUBLANES - 1
    return pl.pallas_call(
        _sconv_kernel,
        grid=(n // tm, ch // tc),
        in_specs=[pl.BlockSpec((tm, tc), lambda i, j: (i, j)),
                  pl.BlockSpec((SUBLANES, tc), lambda i, j: (jnp.maximum(i * per - 1, 0), j)),
                  pl.BlockSpec((SUBLANES, tc), lambda i, j: (jnp.minimum((i + 1) * per, last), j)),
                  pl.BlockSpec((conv_w.shape[0], tc), lambda i, j: (0, j)),
                  pl.BlockSpec((1, tc), lambda i, j: (0, j))],
        out_specs=pl.BlockSpec((tm, tc), lambda i, j: (i, j)),
        out_shape=jax.ShapeDtypeStruct((n, ch), F32),
        compiler_params=_cparams("parallel", "parallel"),
        name="short_conv",
    )(z, z, z, conv_w, conv_b.reshape(1, ch))


def _filt_kernel(bands_ref, w1t_ref, w1c_ref, w1s_ref, b1_ref, w2_ref, b2_ref, w3_ref, b3_ref, fr_ref,
                 woh_ref, wol_ref, dl_ref, g_ref, s_ref, *, seq):
    i = pl.program_id(0)
    tl = g_ref.shape[1]
    rows = tl // FILT_PACK

    def lag_of(m):
        return jnp.where(m < seq, m, 2 * seq - m).astype(F32) / seq

    def packed_t(lanes_per_group):
        shape = (rows, FILT_PACK * lanes_per_group)
        r = lax.broadcasted_iota(jnp.int32, shape, 0)
        grp = lax.shift_right_logical(lax.broadcasted_iota(jnp.int32, shape, 1), int(math.log2(lanes_per_group)))
        return lag_of(i * tl + grp * rows + r)

    ang = ((2.0 * math.pi) * packed_t(HY_BANDS)) * bands_ref[...]
    fr = fr_ref[...]
    pre = (packed_t(HY_FILT_HIDDEN) * w1t_ref[...] + _hdot(jnp.cos(ang), w1c_ref[...])
           + _hdot(-jnp.sin(ang), w1s_ref[...]) + b1_ref[...])
    h = jnp.sin(fr * pre)
    h = jnp.sin(fr * (_hdot(h, w2_ref[...]) + b2_ref[...]))
    h = jnp.sin(fr * (_hdot(h, w3_ref[...]) + b3_ref[...]))
    lane = lax.broadcasted_iota(jnp.int32, (rows, LANES), 1)
    pieces = []
    for gi in range(FILT_PACK):
        slab = h[:, (gi // 2) * LANES:(gi // 2 + 1) * LANES]
        if gi % 2:
            slab = pltpu.roll(slab, HY_FILT_HIDDEN, 1)
        pieces.append(jnp.where(lane < HY_FILT_HIDDEN, slab, 0.0))
    hid = jnp.concatenate(pieces, axis=0)
    m = i * tl + lax.broadcasted_iota(jnp.int32, (tl, 1), 0)
    window = jnp.where(m == seq, 0.0, jnp.exp(-lag_of(m) * dl_ref[...]))

    @pl.when(i == 0)
    def _():
        s_ref[...] = jnp.zeros_like(s_ref)

    hid = _split(hid)
    for o in range(g_ref.shape[0]):
        f = _dot3(hid, (woh_ref[o], wol_ref[o])) * window
        g_ref[o] = f
        s_ref[o] += jnp.sum(jnp.abs(f), axis=0, keepdims=True)


def _hyena_filter(seq, w1, b1, w2, b2, w3, b3, freq, w_out):
    fh = HY_FILT_HIDDEN
    orders = w_out.shape[1]
    w = HY_WIDTH
    tl = min(512, seq)
    assert tl % (FILT_PACK * SUBLANES) == 0 and FILT_PACK * HY_BANDS == LANES
    eye = jnp.eye(FILT_PACK, dtype=F32)
    pad2 = lambda a: jnp.kron(eye, a)
    padv = lambda a: jnp.tile(a.reshape(1, -1), (1, FILT_PACK))
    bands = padv(jnp.linspace(1e-4, HY_BANDS - 1, HY_BANDS, dtype=F32))
    w1t = padv(w1[0])
    w1c = pad2(w1[1:1 + HY_BANDS])
    w1s = pad2(w1[1 + HY_BANDS:])
    wo_hi, wo_lo = _split(jnp.pad(w_out.transpose(1, 2, 0, 3),
                                  ((0, 0), (0, 0), (0, LANES - fh), (0, 0))))
    deltas = jnp.abs(jnp.linspace(HY_DECAY_MIN, HY_DECAY_MAX, w, dtype=F32)).reshape(1, w)
    nt = 2 * seq // tl
    half = seq // tl
    full = lambda a: pl.BlockSpec(a.shape, lambda i: (0,) * a.ndim)
    args = (bands, w1t, w1c, w1s, padv(b1), pad2(w2), padv(b2), pad2(w3), padv(b3), padv(freq))
    return pl.pallas_call(
        functools.partial(_filt_kernel, seq=seq),
        grid=(nt,),
        in_specs=[full(a) for a in args] + [pl.BlockSpec((orders, None, LANES, w), lambda i: (0, i // half, 0, 0)),
                                            pl.BlockSpec((orders, None, LANES, w), lambda i: (0, i // half, 0, 0)),
                                            full(deltas)],
        out_specs=[pl.BlockSpec((orders, tl, w), lambda i: (0, i, 0)),
                   pl.BlockSpec((orders, 1, w), lambda i: (0, 0, 0))],
        out_shape=[jax.ShapeDtypeStruct((orders, 2 * seq, w), F32), jax.ShapeDtypeStruct((orders, 1, w), F32)],
        compiler_params=_cparams("arbitrary"),
        name="hyena_filter",
    )(*args, wo_hi, wo_lo, deltas)


def _dft_consts(seq):
    n = 2 * seq
    n2 = DFT_N2
    n1 = n // n2
    k1 = n1 // 2 + 1
    two_pi = 2.0 * np.pi
    kk, nn = np.meshgrid(np.arange(k1), np.arange(n1), indexing="ij")
    ang_a = two_pi * ((kk * nn) % n1) / n1
    fa = (np.cos(ang_a), -np.sin(ang_a))
    k2, m2 = np.meshgrid(np.arange(n2), np.arange(n2), indexing="ij")
    ang_c = two_pi * ((k2 * m2) % n2) / n2
    fc = (np.cos(ang_c), -np.sin(ang_c))
    kk, mm = np.meshgrid(np.arange(k1), np.arange(n2), indexing="ij")
    ang_t = two_pi * ((kk * mm) % n) / n
    tw = (np.cos(ang_t), -np.sin(ang_t))
    weight = np.where((np.arange(k1) == 0) | (np.arange(k1) == n1 // 2), 1.0, 2.0) / n
    rows, cols = np.meshgrid(np.arange(n1 // 2), np.arange(k1), indexing="ij")
    ang_f = two_pi * ((rows * cols) % n1) / n1
    mc = np.cos(ang_f) * weight[None, :]
    ms = -np.sin(ang_f) * weight[None, :]
    kp = -(-k1 // SUBLANES) * SUBLANES
    rpad = lambda a: np.pad(a, ((0, kp - k1), (0, 0)))
    cpad = lambda a: np.pad(a, ((0, 0), (0, kp - k1)))
    f32 = lambda a: jnp.asarray(a, F32)
    fc3 = jnp.stack([_lhs3(f32(m)) for m in (fc[0], fc[1] - fc[0], fc[0] + fc[1])])
    mcs = jnp.concatenate([f32(cpad(mc)), f32(cpad(ms))], axis=1)
    return {"n1": n1, "k1": k1, "kp": kp,
            "fa_r": f32(rpad(fa[0])), "fa_i": f32(rpad(fa[1])), "fc3": fc3,
            "twa_r": f32(rpad(tw[0]).T[:, :, None]), "twa_i": f32(rpad(tw[1]).T[:, :, None]),
            "twc_r": f32(rpad(tw[0])[:, :, None]), "twc_i": f32(rpad(tw[1])[:, :, None]),
            "mcs3": _lhs3(mcs)}


def _dfta_kernel(x_ref, f_ref, twr_ref, twi_ref, or_ref, oi_ref):
    kp = or_ref.shape[0]
    f = f_ref[...]
    xt = pltpu.einshape("nsw->snw", x_ref[...])
    out_r, out_i = [], []
    for s in range(SUBLANES):
        a = _bdot(f, _rhs3(xt[s]))
        ar, ai = a[:kp], a[kp:]
        tr = twr_ref[s]
        ti = twi_ref[s]
        out_r.append(ar * tr - ai * ti)
        out_i.append(ar * ti + ai * tr)
    or_ref[...] = pltpu.einshape("skw->ksw", jnp.stack(out_r))
    oi_ref[...] = pltpu.einshape("skw->ksw", jnp.stack(out_i))


def _dft_stage_a(x3, col0, dc, n1c, row_blk=0):
    k1 = dc["kp"]
    tw = 1024
    wt = HY_WIDTH // tw
    f3 = _lhs3(jnp.concatenate([dc["fa_r"][:, :n1c], dc["fa_i"][:, :n1c]], axis=0))
    out = jax.ShapeDtypeStruct((k1, DFT_N2, HY_WIDTH), F32)
    return pl.pallas_call(
        _dfta_kernel,
        grid=(DFT_N2 // SUBLANES, wt),
        in_specs=[pl.BlockSpec((n1c, SUBLANES, tw), lambda b, j: (row_blk, b, col0 * wt + j)),
                  pl.BlockSpec(f3.shape, lambda b, j: (0, 0)),
                  pl.BlockSpec((SUBLANES, k1, 1), lambda b, j: (b, 0, 0)),
                  pl.BlockSpec((SUBLANES, k1, 1), lambda b, j: (b, 0, 0))],
        out_specs=[pl.BlockSpec((k1, SUBLANES, tw), lambda b, j: (0, b, j))] * 2,
        out_shape=[out, out],
        compiler_params=_cparams("parallel", "parallel"),
        name="dft_stage_a",
    )(x3, f3, dc["twa_r"], dc["twa_i"])


def _cmul3(f_ref, ar, ai):
    k1 = _bdot(f_ref[0], _rhs3(ar + ai))
    return k1 - _bdot(f_ref[2], _rhs3(ai)), k1 + _bdot(f_ref[1], _rhs3(ar))


def _dftc_filt_kernel(ar_ref, ai_ref, f_ref, s_ref, gr_ref, gi_ref, *, k1):
    @pl.when(pl.program_id(0) < k1)
    def _():
        xr, xi = _cmul3(f_ref, ar_ref[...], ai_ref[...])
        inv = 1.0 / s_ref[...]
        gr_ref[...] = xr * inv
        gi_ref[...] = xi * inv

    @pl.when(pl.program_id(0) >= k1)
    def _():
        gr_ref[...] = jnp.zeros_like(gr_ref)
        gi_ref[...] = jnp.zeros_like(gi_ref)


def _dft_filter_spectrum(ar, ai, abs_sum, dc):
    kp = dc["kp"]
    tw = 1024
    blk = pl.BlockSpec((None, DFT_N2, tw), lambda k, j: (k, 0, j))
    mat = pl.BlockSpec(dc["fc3"].shape, lambda k, j: (0, 0, 0))
    out = jax.ShapeDtypeStruct((kp, DFT_N2, HY_WIDTH), F32)
    return pl.pallas_call(
        functools.partial(_dftc_filt_kernel, k1=dc["k1"]),
        grid=(kp, HY_WIDTH // tw),
        in_specs=[blk, blk, mat, pl.BlockSpec((1, tw), lambda k, j: (0, j))],
        out_specs=[blk, blk],
        out_shape=[out, out],
        compiler_params=_cparams("parallel", "parallel"),
        name="dft_filter_spectrum",
    )(ar, ai, dc["fc3"], abs_sum)


def _dftmid_kernel(ar_ref, ai_ref, gr_ref, gi_ref, f_ref, twr_ref, twi_ref, or_ref, oi_ref, *, k1):
    @pl.when(pl.program_id(0) < k1)
    def _():
        xr, xi = _cmul3(f_ref, ar_ref[...], ai_ref[...])
        gr = gr_ref[...]
        gi = gi_ref[...]
        yr = xr * gr - xi * gi
        yi = xr * gi + xi * gr
        k1_ = _bdot(f_ref[0], _rhs3(yr + yi))
        br = k1_ + _bdot(f_ref[1], _rhs3(yi))
        bi = k1_ - _bdot(f_ref[2], _rhs3(yr))
        tr = twr_ref[...]
        ti = twi_ref[...]
        or_ref[...] = br * tr + bi * ti
        oi_ref[...] = bi * tr - br * ti

    @pl.when(pl.program_id(0) >= k1)
    def _():
        or_ref[...] = jnp.zeros_like(or_ref)
        oi_ref[...] = jnp.zeros_like(oi_ref)


def _dft_middle(ar, ai, gr, gi, dc):
    kp = dc["kp"]
    tw = 1024
    blk = pl.BlockSpec((None, DFT_N2, tw), lambda k, j: (k, 0, j))
    mat = pl.BlockSpec(dc["fc3"].shape, lambda k, j: (0, 0, 0))
    twd = pl.BlockSpec((None, DFT_N2, 1), lambda k, j: (k, 0, 0))
    out = jax.ShapeDtypeStruct((kp, DFT_N2, HY_WIDTH), F32)
    return pl.pallas_call(
        functools.partial(_dftmid_kernel, k1=dc["k1"]),
        grid=(kp, HY_WIDTH // tw),
        in_specs=[blk, blk, blk, blk, mat, twd, twd],
        out_specs=[blk, blk],
        out_shape=[out, out],
        compiler_params=_cparams("parallel", "parallel"),
        name="dft_middle",
    )(ar, ai, gr, gi, dc["fc3"], dc["twc_r"], dc["twc_i"])


def _dftf_kernel(br_ref, bi_ref, m_ref, gate_ref, y_ref, skip_ref, o_ref):
    m = m_ref[...]
    brt = pltpu.einshape("ksw->skw", br_ref[...])
    bit = pltpu.einshape("ksw->skw", bi_ref[...])
    conv = [_bdot(m, _rhs3(jnp.concatenate([brt[s], bit[s]], axis=0))) for s in range(SUBLANES)]
    conv = pltpu.einshape("snw->nsw", jnp.stack(conv))
    o_ref[...] = gate_ref[...] * (conv + skip_ref[...] * y_ref[...])


def _dft_final(br, bi, dc, gate3, gate_col, y3, y_col, skip):
    k1 = dc["kp"]
    nh = dc["n1"] // 2
    tw = 1024
    wt = HY_WIDTH // tw
    bblk = pl.BlockSpec((k1, SUBLANES, tw), lambda b, j: (0, b, j))
    mat = pl.BlockSpec(dc["mcs3"].shape, lambda b, j: (0, 0))
    return pl.pallas_call(
        _dftf_kernel,
        grid=(DFT_N2 // SUBLANES, wt),
        in_specs=[bblk, bblk, mat,
                  pl.BlockSpec((nh, SUBLANES, tw), lambda b, j: (0, b, gate_col * wt + j)),
                  pl.BlockSpec((nh, SUBLANES, tw), lambda b, j: (0, b, y_col * wt + j)),
                  pl.BlockSpec((1, tw), lambda b, j: (0, j))],
        out_specs=pl.BlockSpec((nh, SUBLANES, tw), lambda b, j: (0, b, j)),
        out_shape=jax.ShapeDtypeStruct((nh, DFT_N2, HY_WIDTH), F32),
        compiler_params=_cparams("parallel", "parallel"),
        name="dft_final",
    )(br, bi, dc["mcs3"], gate3, y3, skip)


def _hyena_mixer(x, shift, scale, gate, g_norm, w_in, conv_w, conv_b, w1, b1, w2, b2, w3, b3, freq,
                 filt_w_out, skip, w_out):
    seq = x.shape[0]
    dc = _dft_consts(seq)
    nh = dc["n1"] // 2
    z = _norm_proj(x, shift, scale, g_norm, w_in)
    zc = _short_conv(z, conv_w, conv_b)
    zc3 = zc.reshape(nh, DFT_N2, 3 * HY_WIDTH)
    y3, y_col = zc3, 2
    g_all, abs_sums = _hyena_filter(seq, w1, b1, w2, b2, w3, b3, freq, filt_w_out)
    for o in range(filt_w_out.shape[1]):
        ga_r, ga_i = _dft_stage_a(g_all.reshape(-1, DFT_N2, HY_WIDTH), 0, dc, dc["n1"], row_blk=o)
        gr, gi = _dft_filter_spectrum(ga_r, ga_i, abs_sums[o], dc)
        ya_r, ya_i = _dft_stage_a(y3, y_col, dc, nh)
        br, bi = _dft_middle(ya_r, ya_i, gr, gi, dc)
        y3 = _dft_final(br, bi, dc, zc3, o, y3, y_col, skip[o].reshape(1, HY_WIDTH))
        y_col = 0
    return _resid_proj(x, y3.reshape(seq, HY_WIDTH), gate, w_out)


def _even_mixer(x, cx, mx, mc, g_norm, ev_w_in, q_a_norm_g, w_uq, kv_a_norm_g, w_ukv, q_head_g, k_head_g,
                lam_re, lam_im, log_dt, b_re, b_im, c_re, c_im, d_skip, glu_w, glu_b, ev_w_out):
    n = x.shape[0]
    nc = cx.shape[0]
    cuts = (Q_LORA, Q_LORA + KV_LORA, Q_LORA + KV_LORA + QK_ROPE)
    w_qa, w_kva, w_kpe, w_u = (ev_w_in[:, :cuts[0]], ev_w_in[:, cuts[0]:cuts[1]],
                               ev_w_in[:, cuts[1]:cuts[2]], ev_w_in[:, cuts[2]:])
    w_all = jnp.concatenate([w_u, w_qa, w_kva, _pad_lanes(w_kpe), _pad_lanes(w_kpe[:, _ROPE_SWAP])], axis=1)
    proj_x = _norm_proj(x, mx[3], mx[4], g_norm, w_all)
    proj_c = _norm_proj(cx, mc[3], mc[4], g_norm, w_all)

    wq3, wkv, gq3, gk3 = _mla_weights(w_uq, w_ukv, q_head_g, k_head_g)
    gqa = q_a_norm_g.reshape(1, Q_LORA)
    gkva = kv_a_norm_g.reshape(1, KV_LORA)
    cos_x, sin_x = _rope_tables(n)
    cos_c = _pad_lanes(jnp.ones((nc, QK_ROPE), F32))
    sin_c = jnp.zeros((nc, LANES), F32)
    q_x, k_x, v_x = _mla_heads(proj_x, cos_x, sin_x, gqa, gkva, wq3, wkv, gq3, gk3)
    _, k_c, v_c = _mla_heads(proj_c, cos_c, sin_c, gqa, gkva, wq3, wkv, gq3, gk3)
    att = _attention(q_x, jnp.concatenate([k_x, k_c], axis=1), jnp.concatenate([v_x, v_c], axis=1))

    d_row = d_skip.reshape(1, S5_WIDTH)
    zero = jnp.zeros((SUBLANES, S5_NSTATE), F32)
    y = None
    for direction, reverse in ((0, False), (1, True)):
        prm = _s5_params(lam_re[direction], lam_im[direction], log_dt[direction], b_re[direction],
                         b_im[direction], c_re[direction], c_im[direction], reverse)
        _, hc_r, hc_i = _s5_scan(proj_c, d_row, (zero, zero), prm, reverse=reverse, first=True)
        if y is None:
            y, _, _ = _s5_scan(proj_x, d_row, (hc_r, hc_i), prm, reverse=reverse, first=True)
        else:
            y, _, _ = _s5_scan(proj_x, y, (hc_r, hc_i), prm, reverse=reverse, first=False)
    return _even_out(x, att, y, mx[5], glu_w, glu_b, ev_w_out)


def kernel(x, c, ctx, c_ctx, ada_w, ada_b, norm_g, ffn_w_in, ffn_w_out, ev_w_in, mla_q_a_norm_g, mla_w_uq, mla_kv_a_norm_g, mla_w_ukv, mla_q_head_g, mla_k_head_g, s5_lam_re, s5_lam_im, s5_log_dt, s5_b_re, s5_b_im, s5_c_re, s5_c_im, s5_d, s5_glu_w, s5_glu_b, ev_w_out, hy_w_in, hy_conv_w, hy_conv_b, hy_filt_w1, hy_filt_b1, hy_filt_w2, hy_filt_b2, hy_filt_w3, hy_filt_b3, hy_filt_freq, hy_filt_w_out, hy_skip, hy_w_out):
    assert x.shape[0] == 1 and ctx.shape[0] == 1, "kernels are written for batch 1"
    depth = ada_w.shape[0]
    mods = _ada_mods(c, c_ctx, ada_w, ada_b)
    xs = x[0]
    cx = ctx[0]
    for i in range(depth):
        even = i % 2 == 0
        li = i // 2
        need_after = any(j % 2 == 0 for j in range(i + 1, depth))
        use_ctx = even or need_after
        assert not need_after, "context outputs after an even layer are not needed at this depth"
        mx = mods[i, 0]
        mc = mods[i, 1]
        xs = _ffn_half(xs, mx, 0, norm_g[i, 0], ffn_w_in[i, 0], ffn_w_out[i, 0])
        if use_ctx:
            cx = _ffn_half(cx, mc, 0, norm_g[i, 0], ffn_w_in[i, 0], ffn_w_out[i, 0])
        if even:
            xs = _even_mixer(xs, cx, mx, mc, norm_g[i, 1], ev_w_in[li], mla_q_a_norm_g[li], mla_w_uq[li],
                             mla_kv_a_norm_g[li], mla_w_ukv[li], mla_q_head_g[li], mla_k_head_g[li],
                             s5_lam_re[li], s5_lam_im[li], s5_log_dt[li], s5_b_re[li], s5_b_im[li],
                             s5_c_re[li], s5_c_im[li], s5_d[li], s5_glu_w[li], s5_glu_b[li], ev_w_out[li])
        else:
            xs = _hyena_mixer(xs, mx[3], mx[4], mx[5], norm_g[i, 1], hy_w_in[li], hy_conv_w[li], hy_conv_b[li],
                              hy_filt_w1[li], hy_filt_b1[li], hy_filt_w2[li], hy_filt_b2[li], hy_filt_w3[li],
                              hy_filt_b3[li], hy_filt_freq[li], hy_filt_w_out[li], hy_skip[li], hy_w_out[li])
        xs = _ffn_half(xs, mx, 6, norm_g[i, 2], ffn_w_in[i, 1], ffn_w_out[i, 1])
    return xs[None]
```

```python
import functools
import math

import numpy as np
import jax
import jax.numpy as jnp
from jax import lax
from jax.experimental import pallas as pl
from jax.experimental.pallas import tpu as pltpu

F32 = jnp.float32
BF16 = jnp.bfloat16
HIGHEST = lax.Precision.HIGHEST

D_MODEL = 2048
N_MOD = 9
D_FF = 5632
NORM_EPS = 1e-6
GRID_W = 64
MLA_HEADS = 8
QK_NOPE = 128
QK_ROPE = 64
QK_DIM = QK_NOPE + QK_ROPE
V_DIM = 128
Q_LORA = 512
KV_LORA = 256
ROPE_BASE = 10000.0
S5_WIDTH = 1024
S5_GROUP = 16
S5_GROUPS = S5_WIDTH // S5_GROUP
S5_STATE = 64
S5_NSTATE = S5_GROUPS * S5_STATE
S5_CHUNKS = 4
HY_WIDTH = D_MODEL
HY_BANDS = 16
HY_FILT_HIDDEN = 64
HY_DECAY_MIN = math.log(1e-2) / 1.5
HY_DECAY_MAX = math.log(1e-2) / 0.3
DFT_N2 = 256
FILT_PACK = 8
LANES = 128
SUBLANES = 8
VMEM_LIMIT = 56 * 1024 * 1024


def _cparams(*sem):
    return pltpu.CompilerParams(dimension_semantics=sem, vmem_limit_bytes=VMEM_LIMIT)


def _hdot(a, b):
    return jnp.dot(a, b, preferred_element_type=F32, precision=HIGHEST)


def _bdot(a, b):
    return jnp.dot(a, b, preferred_element_type=F32)


def _split(x):
    hi = x.astype(BF16)
    return hi, (x - hi.astype(F32)).astype(BF16)


def _dot3(a, b):
    return _bdot(a[0], b[0]) + (_bdot(a[1], b[0]) + _bdot(a[0], b[1]))


def _lhs3(a):
    hi, lo = _split(a)
    return jnp.concatenate([hi, lo, hi], axis=1)


def _rhs3(b):
    hi, lo = _split(b)
    return jnp.concatenate([hi, hi, lo], axis=0)


def _norm_mod(x, g, scale, shift):
    y = x * lax.rsqrt(jnp.mean(x * x, axis=-1, keepdims=True) + NORM_EPS) * g
    return y * (1.0 + scale) + shift


def _mods_kernel(s_ref, w_ref, b_ref, o_ref):
    s = s_ref[...]
    s = s * jax.nn.sigmoid(s)
    o_ref[...] = _hdot(s, w_ref[...]) + b_ref[...]


def _ada_mods(c, c_ctx, ada_w, ada_b):
    depth, d, nd = ada_w.shape
    s = jnp.zeros((SUBLANES, d), F32).at[0].set(c[0]).at[1].set(c_ctx)
    tn = 1024
    out = pl.pallas_call(
        _mods_kernel,
        grid=(depth, nd // tn),
        in_specs=[pl.BlockSpec((SUBLANES, d), lambda l, j: (0, 0)),
                  pl.BlockSpec((None, d, tn), lambda l, j: (l, 0, j)),
                  pl.BlockSpec((None, 1, tn), lambda l, j: (l, 0, j))],
        out_specs=pl.BlockSpec((None, SUBLANES, tn), lambda l, j: (l, 0, j)),
        out_shape=jax.ShapeDtypeStruct((depth, SUBLANES, nd), F32),
        compiler_params=_cparams("parallel", "parallel"),
        name="ada_mods",
    )(s, ada_w, ada_b.reshape(depth, 1, nd))
    return out[:, :2].reshape(depth, 2, N_MOD, 1, d)


FFN_NORM_SLICES = 8


def _ffn_kernel(x_ref, xn_ref, sh_ref, sc_ref, gt_ref, g_ref, wgu_ref, wo_ref, o_ref,
                h0_scr, h1_scr, acc_scr):
    i = pl.program_id(0)
    j = pl.program_id(1)
    rows = x_ref.shape[0] // FFN_NORM_SLICES
    tf = wo_ref.shape[0]

    def norm(x):
        return _norm_mod(x, g_ref[...], sc_ref[...], sh_ref[...]).astype(BF16)

    @pl.when((i == 0) & (j == 0))
    def _():
        h0_scr[...] = norm(x_ref[...])

    @pl.when(j == 0)
    def _():
        acc_scr[...] = jnp.zeros_like(acc_scr)

    def step(h_cur, h_nxt):
        r0 = pl.multiple_of(jnp.minimum(j, FFN_NORM_SLICES - 1) * rows, rows)
        h_nxt[pl.ds(r0, rows), :] = norm(xn_ref[pl.ds(r0, rows), :])
        gu = _bdot(h_cur[...], wgu_ref[...])
        g, u = gu[:, :tf], gu[:, tf:]
        a = (g * jax.nn.sigmoid(g)) * u
        acc_scr[...] += _bdot(a.astype(BF16), wo_ref[...])

    @pl.when(lax.rem(i, 2) == 0)
    def _():
        step(h0_scr, h1_scr)

    @pl.when(lax.rem(i, 2) == 1)
    def _():
        step(h1_scr, h0_scr)

    @pl.when(j == pl.num_programs(1) - 1)
    def _():
        o_ref[...] = x_ref[...] + gt_ref[...] * (0.5 * acc_scr[...])


def _ffn_half(x, mod, k0, g_norm, w_in, w_out):
    n, d = x.shape
    f = w_out.shape[0]
    tm = min(512, n)
    tf = 512
    nf = f // tf
    nt = n // tm
    assert nf >= FFN_NORM_SLICES and tm % (FFN_NORM_SLICES * 2 * SUBLANES) == 0
    vec = pl.BlockSpec((1, d), lambda i, j: (0, 0))
    w_gu = w_in.astype(BF16).reshape(d, 2, nf, tf).transpose(2, 0, 1, 3).reshape(nf, d, 2 * tf)
    return pl.pallas_call(
        _ffn_kernel,
        grid=(nt, nf),
        in_specs=[pl.BlockSpec((tm, d), lambda i, j: (i, 0)),
                  pl.BlockSpec((tm, d), lambda i, j: (jnp.minimum(i + 1, nt - 1), 0)),
                  vec, vec, vec, vec,
                  pl.BlockSpec((None, d, 2 * tf), lambda i, j: (j, 0, 0)),
                  pl.BlockSpec((tf, d), lambda i, j: (j, 0))],
        out_specs=pl.BlockSpec((tm, d), lambda i, j: (i, 0)),
        out_shape=jax.ShapeDtypeStruct((n, d), F32),
        scratch_shapes=[pltpu.VMEM((tm, d), BF16), pltpu.VMEM((tm, d), BF16), pltpu.VMEM((tm, d), F32)],
        compiler_params=_cparams("arbitrary", "arbitrary"),
        name="ffn_half",
    )(x, x, mod[k0], mod[k0 + 1], mod[k0 + 2], g_norm.reshape(1, d), w_gu, w_out.astype(BF16))


def _proj_kernel(x_ref, sh_ref, sc_ref, g_ref, w_ref, o_ref, h_scr):
    @pl.when(pl.program_id(1) == 0)
    def _():
        h_scr[...] = _norm_mod(x_ref[...], g_ref[...], sc_ref[...], sh_ref[...]).astype(BF16)

    o_ref[...] = _bdot(h_scr[...], w_ref[...])


def _norm_proj(x, shift, scale, g_norm, w):
    n, d = x.shape
    nout = w.shape[1]
    tm = min(512, n)
    tn = 2048
    vec = pl.BlockSpec((1, d), lambda i, j: (0, 0))
    return pl.pallas_call(
        _proj_kernel,
        grid=(n // tm, nout // tn),
        in_specs=[pl.BlockSpec((tm, d), lambda i, j: (i, 0)), vec, vec, vec,
                  pl.BlockSpec((d, tn), lambda i, j: (0, j))],
        out_specs=pl.BlockSpec((tm, tn), lambda i, j: (i, j)),
        out_shape=jax.ShapeDtypeStruct((n, nout), F32),
        scratch_shapes=[pltpu.VMEM((tm, d), BF16)],
        compiler_params=_cparams("parallel", "arbitrary"),
        name="norm_proj",
    )(x, shift, scale, g_norm.reshape(1, d), w.astype(BF16))


def _mla_kernel(qa_ref, kva_ref, kpe_ref, cos_ref, sin_ref, gqa_ref, gkva_ref, wq_ref, wkv_ref,
                gq_ref, gk_ref, q_ref, k_ref, v_ref):
    def rms(t, g):
        return t * lax.rsqrt(jnp.mean(t * t, axis=-1, keepdims=True) + NORM_EPS) * g

    qn = rms(qa_ref[...], gqa_ref[...]).astype(BF16)
    kvn = rms(kva_ref[...], gkva_ref[...]).astype(BF16)
    cos = cos_ref[...]
    sin = sin_ref[...]
    gq = gq_ref[...]
    gk = gk_ref[...]
    kpe = kpe_ref[:, :LANES]
    kpe_sw = kpe_ref[:, LANES:]
    kpe_ss = jnp.sum(kpe * kpe, axis=-1, keepdims=True)
    k_rot = (kpe * gk[:, LANES:2 * LANES]) * cos + (kpe_sw * gk[:, 2 * LANES:]) * sin
    inv_dim = 1.0 / QK_DIM
    q_scale = math.log2(math.e) / math.sqrt(QK_DIM)
    for h in range(MLA_HEADS):
        qh = _bdot(qn, wq_ref[h])
        nope, pe, pe_sw = qh[:, :LANES], qh[:, LANES:2 * LANES], qh[:, 2 * LANES:]
        ss = jnp.sum(nope * nope, axis=-1, keepdims=True) + jnp.sum(pe * pe, axis=-1, keepdims=True)
        r = lax.rsqrt(ss * inv_dim + NORM_EPS) * q_scale
        q_rot = (pe * gq[:, LANES:2 * LANES]) * cos + (pe_sw * gq[:, 2 * LANES:]) * sin
        q_ref[h, :, :LANES] = (nope * r * gq[:, :LANES]).astype(BF16)
        q_ref[h, :, LANES:] = (q_rot * r).astype(BF16)
        kv = _bdot(kvn, wkv_ref[h])
        k_nope, vv = kv[:, :LANES], kv[:, LANES:]
        ssk = jnp.sum(k_nope * k_nope, axis=-1, keepdims=True) + kpe_ss
        rk = lax.rsqrt(ssk * inv_dim + NORM_EPS)
        k_ref[h, :, :LANES] = (k_nope * rk * gk[:, :LANES]).astype(BF16)
        k_ref[h, :, LANES:] = (k_rot * rk).astype(BF16)
        v_ref[h] = vv.astype(BF16)


_ROPE_SWAP = np.concatenate([np.arange(16, 32), np.arange(0, 16), np.arange(48, 64), np.arange(32, 48)])


def _pad_lanes(a, width=LANES):
    return jnp.pad(a, [(0, 0)] * (a.ndim - 1) + [(0, width - a.shape[-1])])


def _mla_weights(w_uq, w_ukv, q_head_g, k_head_g):
    wq = w_uq.reshape(Q_LORA, MLA_HEADS, QK_DIM).transpose(1, 0, 2)
    wq_pe = wq[:, :, QK_NOPE:]
    wq3 = jnp.concatenate([wq[:, :, :QK_NOPE], _pad_lanes(wq_pe), _pad_lanes(wq_pe[:, :, _ROPE_SWAP])], axis=-1)
    wkv = w_ukv.reshape(KV_LORA, MLA_HEADS, QK_NOPE + V_DIM).transpose(1, 0, 2)

    def g3(g):
        pe = g[QK_NOPE:]
        return jnp.concatenate([g[:QK_NOPE], _pad_lanes(pe), _pad_lanes(pe[_ROPE_SWAP])]).reshape(1, 3 * LANES)

    return wq3.astype(BF16), wkv.astype(BF16), g3(q_head_g), g3(k_head_g)


def _rope_tables(n):
    row = jnp.repeat(jnp.arange(n // GRID_W, dtype=F32), GRID_W)
    col = jnp.tile(jnp.arange(GRID_W, dtype=F32), n // GRID_W)
    n_freq = QK_ROPE // 4
    inv_freq = ROPE_BASE ** (-jnp.arange(n_freq, dtype=F32) / n_freq)
    ang_r = row[:, None] * inv_freq
    ang_c = col[:, None] * inv_freq
    cr, sr, cc, sc = jnp.cos(ang_r), jnp.sin(ang_r), jnp.cos(ang_c), jnp.sin(ang_c)
    cos = jnp.concatenate([cr, cr, cc, cc], axis=-1)
    sin = jnp.concatenate([-sr, sr, -sc, sc], axis=-1)
    return _pad_lanes(cos), _pad_lanes(sin)


def _mla_heads(proj, cos, sin, gqa, gkva, wq3, wkv, gq3, gk3):
    n = proj.shape[0]
    tm = min(512, n)
    hd = 2 * LANES
    full = lambda a: pl.BlockSpec(a.shape, lambda i: (0,) * a.ndim)
    return pl.pallas_call(
        _mla_kernel,
        grid=(n // tm,),
        in_specs=[pl.BlockSpec((tm, Q_LORA), lambda i: (i, S5_WIDTH // Q_LORA)),
                  pl.BlockSpec((tm, KV_LORA), lambda i: (i, (S5_WIDTH + Q_LORA) // KV_LORA)),
                  pl.BlockSpec((tm, 2 * LANES), lambda i: (i, (S5_WIDTH + Q_LORA + KV_LORA) // (2 * LANES))),
                  pl.BlockSpec((tm, LANES), lambda i: (i, 0)),
                  pl.BlockSpec((tm, LANES), lambda i: (i, 0)),
                  full(gqa), full(gkva), full(wq3), full(wkv), full(gq3), full(gk3)],
        out_specs=[pl.BlockSpec((MLA_HEADS, tm, hd), lambda i: (0, i, 0)),
                   pl.BlockSpec((MLA_HEADS, tm, hd), lambda i: (0, i, 0)),
                   pl.BlockSpec((MLA_HEADS, tm, V_DIM), lambda i: (0, i, 0))],
        out_shape=[jax.ShapeDtypeStruct((MLA_HEADS, n, hd), BF16),
                   jax.ShapeDtypeStruct((MLA_HEADS, n, hd), BF16),
                   jax.ShapeDtypeStruct((MLA_HEADS, n, V_DIM), BF16)],
        compiler_params=_cparams("parallel"),
        name="mla_heads",
    )(proj, proj, proj, cos, sin, gqa, gkva, wq3, wkv, gq3, gk3)


def _attn_kernel(q_ref, k_ref, v_ref, o_ref, s_scr, bm_scr, m_scr, acc_scr, *, nkv):
    t = pl.program_id(0)
    tk = k_ref.shape[0]
    jp = lax.rem(t + (nkv - 1), nkv)

    @pl.when(t == 0)
    def _():
        s_scr[1] = jnp.zeros(s_scr.shape[1:], F32)
        bm_scr[1] = jnp.zeros(bm_scr.shape[1:], F32)
        m_scr[...] = jnp.zeros_like(m_scr)
        acc_scr[...] = jnp.zeros_like(acc_scr)

    def step(cur, prev):
        s_new = lax.dot_general(q_ref[...], k_ref[...], (((1,), (1,)), ((), ())), preferred_element_type=F32)
        s_scr[cur] = s_new
        bm_scr[cur] = jnp.max(s_new, axis=-1, keepdims=True)
        m_prev = jnp.where(jp == 0, -jnp.inf, m_scr[...])
        m_new = jnp.maximum(m_prev, bm_scr[prev])
        alpha = jnp.exp2(m_prev - m_new)
        p = jnp.exp2(s_scr[prev] - m_new).astype(BF16)
        v_ext = jnp.concatenate([v_ref[...], jnp.ones((tk, LANES), BF16)], axis=1)
        acc_scr[...] = alpha * acc_scr[...] + _bdot(p, v_ext)
        m_scr[...] = m_new

    @pl.when(lax.rem(t, 2) == 0)
    def _():
        step(0, 1)

    @pl.when(lax.rem(t, 2) == 1)
    def _():
        step(1, 0)

    @pl.when((jp == nkv - 1) & (t > 0))
    def _():
        acc = acc_scr[...]
        o_ref[...] = acc[:, :V_DIM] / acc[:, V_DIM:]


def _kv_tile(m):
    for t in (3328, 1280, 1024, 768, 512, 256, 128):
        if m % t == 0:
            return t
    raise ValueError(f"key length {m} is not a multiple of {LANES}")


def _attention(q, k, v):
    h, n, hd = q.shape
    m = k.shape[1]
    tq = min(1024, n)
    tk = _kv_tile(m)
    nq = n // tq
    nkv = m // tk
    total = h * nq * nkv

    def qk_idx(t):
        tt = jnp.minimum(t, total - 1)
        return tt // (nq * nkv), (tt // nkv) % nq, tt % nkv

    def pv_idx(t):
        tt = jnp.maximum(t - 1, 0)
        return tt // (nq * nkv), (tt // nkv) % nq, tt % nkv

    return pl.pallas_call(
        functools.partial(_attn_kernel, nkv=nkv),
        grid=(total + 1,),
        in_specs=[pl.BlockSpec((None, tq, hd), lambda t: (qk_idx(t)[0], qk_idx(t)[1], 0)),
                  pl.BlockSpec((None, tk, hd), lambda t: (qk_idx(t)[0], qk_idx(t)[2], 0)),
                  pl.BlockSpec((None, tk, V_DIM), lambda t: (pv_idx(t)[0], pv_idx(t)[2], 0))],
        out_specs=pl.BlockSpec((None, tq, V_DIM), lambda t: (pv_idx(t)[0], pv_idx(t)[1], 0)),
        out_shape=jax.ShapeDtypeStruct((h, n, V_DIM), F32),
        scratch_shapes=[pltpu.VMEM((2, tq, tk), F32), pltpu.VMEM((2, tq, 1), F32),
                        pltpu.VMEM((tq, 1), F32), pltpu.VMEM((tq, V_DIM + LANES), F32)],
        compiler_params=_cparams("arbitrary"),
        name="flash_attention",
    )(q, k, v)


S5_TILE = 256
S5_LANE_CHUNK = 512


def _s5_kernel(*refs, reverse, first):
    if first:
        (u_ref, d_ref, h0r_ref, h0i_ref, br_ref, bi_ref, cr_ref, ci_ref, tab_ref,
         y_ref, hr_ref, hi_ref, xr_scr, xi_scr) = refs
    else:
        (u_ref, yin_ref, h0r_ref, h0i_ref, br_ref, bi_ref, cr_ref, ci_ref, tab_ref,
         y_ref, hr_ref, hi_ref, xr_scr, xi_scr) = refs
    tm = u_ref.shape[0]
    nblk = tm // SUBLANES
    cin = S5_WIDTH // S5_CHUNKS
    cst = S5_NSTATE // S5_CHUNKS

    @pl.when(pl.program_id(0) == 0)
    def _():
        hr_ref[...] = h0r_ref[...]
        hi_ref[...] = h0i_ref[...]

    for q in range(S5_CHUNKS):
        uq = u_ref[:, q * cin:(q + 1) * cin].astype(BF16)
        xr_scr[:, q * cst:(q + 1) * cst] = _bdot(uq, br_ref[q])
        xi_scr[:, q * cst:(q + 1) * cst] = _bdot(uq, bi_ref[q])

    edge = 0 if reverse else SUBLANES - 1
    for jc in range(S5_NSTATE // S5_LANE_CHUNK):
        sl = slice(jc * S5_LANE_CHUNK, (jc + 1) * S5_LANE_CHUNK)

        def body(r, carry, sl=sl):
            car, cai = carry
            blk = (nblk - 1 - r) if reverse else r
            row = pl.multiple_of(blk * SUBLANES, SUBLANES)
            xr = xr_scr[pl.ds(row, SUBLANES), sl]
            xi = xi_scr[pl.ds(row, SUBLANES), sl]
            for idx, dist in enumerate((1, 2, 4)):
                ar = tab_ref[2 * idx, :, sl]
                ai = tab_ref[2 * idx + 1, :, sl]
                shift = SUBLANES - dist if reverse else dist
                sr = pltpu.roll(xr, shift, 0)
                si = pltpu.roll(xi, shift, 0)
                xr, xi = xr + ar * sr - ai * si, xi + ar * si + ai * sr
            pr = tab_ref[6, :, sl]
            pi = tab_ref[7, :, sl]
            xr, xi = xr + pr * car - pi * cai, xi + pr * cai + pi * car
            xr_scr[pl.ds(row, SUBLANES), sl] = xr
            xi_scr[pl.ds(row, SUBLANES), sl] = xi
            shape = (SUBLANES, S5_LANE_CHUNK)
            return (jnp.broadcast_to(xr[edge:edge + 1, :], shape), jnp.broadcast_to(xi[edge:edge + 1, :], shape))

        car, cai = lax.fori_loop(0, nblk, body, (hr_ref[:, sl], hi_ref[:, sl]), unroll=True)
        hr_ref[:, sl] = car
        hi_ref[:, sl] = cai

    cout = S5_WIDTH // S5_CHUNKS
    for q in range(S5_CHUNKS):
        hr = xr_scr[:, q * cst:(q + 1) * cst].astype(BF16)
        hi = xi_scr[:, q * cst:(q + 1) * cst].astype(BF16)
        y = _bdot(hr, cr_ref[q]) + _bdot(hi, ci_ref[q])
        cs = slice(q * cout, (q + 1) * cout)
        if first:
            base = u_ref[:, cs] * d_ref[:, cs]
        else:
            base = yin_ref[:, cs]
        y_ref[:, cs] = base + y


def _s5_scan(u_src, extra, h0, prm, *, reverse, first):
    n = u_src.shape[0]
    tm = min(S5_TILE, n)
    nt = n // tm
    tmap = (lambda i: (nt - 1 - i, 0)) if reverse else (lambda i: (i, 0))
    full = lambda a: pl.BlockSpec(a.shape, lambda i: (0,) * a.ndim)
    extra_spec = full(extra) if first else pl.BlockSpec((tm, S5_WIDTH), tmap)
    st = jax.ShapeDtypeStruct((SUBLANES, S5_NSTATE), F32)
    st_spec = pl.BlockSpec((SUBLANES, S5_NSTATE), lambda i: (0, 0))
    return pl.pallas_call(
        functools.partial(_s5_kernel, reverse=reverse, first=first),
        grid=(nt,),
        in_specs=[pl.BlockSpec((tm, S5_WIDTH), tmap), extra_spec, st_spec, st_spec,
                  full(prm["br"]), full(prm["bi"]), full(prm["cr"]), full(prm["ci"]), full(prm["tab"])],
        out_specs=[pl.BlockSpec((tm, S5_WIDTH), tmap), st_spec, st_spec],
        out_shape=[jax.ShapeDtypeStruct((n, S5_WIDTH), F32), st, st],
        scratch_shapes=[pltpu.VMEM((tm, S5_NSTATE), F32), pltpu.VMEM((tm, S5_NSTATE), F32)],
        compiler_params=_cparams("arbitrary"),
        name="s5_scan_rev" if reverse else "s5_scan_fwd",
    )(u_src, extra, h0[0], h0[1], prm["br"], prm["bi"], prm["cr"], prm["ci"], prm["tab"])


def _s5_params(lam_re, lam_im, log_dt, b_re, b_im, c_re, c_im, reverse):
    dt = jnp.exp(log_dt)[:, None]
    mag = jnp.exp(lam_re * dt)
    a_re = mag * jnp.cos(lam_im * dt)
    a_im = mag * jnp.sin(lam_im * dt)
    den = lam_re * lam_re + lam_im * lam_im
    nr = a_re - 1.0
    k_re = (nr * lam_re + a_im * lam_im) / den
    k_im = (a_im * lam_re - nr * lam_im) / den
    bx_re = k_re[:, :, None] * b_re - k_im[:, :, None] * b_im
    bx_im = k_re[:, :, None] * b_im + k_im[:, :, None] * b_re
    gpc = S5_GROUPS // S5_CHUNKS
    eye = jnp.eye(gpc, dtype=F32)

    def bdiag_in(b):
        b = b.reshape(S5_CHUNKS, gpc, S5_STATE, S5_GROUP)
        return jnp.einsum('qgpc,gh->qgchp', b, eye).reshape(S5_CHUNKS, gpc * S5_GROUP, gpc * S5_STATE)

    def bdiag_out(c):
        c = c.reshape(S5_CHUNKS, gpc, S5_GROUP, S5_STATE)
        return jnp.einsum('qgcp,gh->qgphc', c, eye).reshape(S5_CHUNKS, gpc * S5_STATE, gpc * S5_GROUP)

    pr, pi = [a_re.reshape(-1)], [a_im.reshape(-1)]
    for _ in range(SUBLANES - 1):
        pr, pi = pr + [pr[-1] * pr[0] - pi[-1] * pi[0]], pi + [pr[-1] * pi[0] + pi[-1] * pr[0]]
    t = np.arange(SUBLANES)
    tabs = []
    for dist in (1, 2, 4):
        keep = (t <= SUBLANES - 1 - dist) if reverse else (t >= dist)
        mask = jnp.asarray(keep, F32)[:, None]
        tabs += [mask * pr[dist - 1][None, :], mask * pi[dist - 1][None, :]]
    order = [SUBLANES - 1 - i for i in range(SUBLANES)] if reverse else list(range(SUBLANES))
    tabs += [jnp.stack([pr[i] for i in order]), jnp.stack([pi[i] for i in order])]
    return {"br": bdiag_in(bx_re).astype(BF16), "bi": bdiag_in(bx_im).astype(BF16),
            "cr": bdiag_out(c_re).astype(BF16), "ci": bdiag_out(-c_im).astype(BF16),
            "tab": jnp.stack(tabs)}


def _gelu(x):
    return x * (0.5 * (1.0 + jnp.tanh(math.sqrt(2.0 / math.pi) * (x + 0.044715 * (x * x * x)))))


def _evout_kernel(x_ref, att_ref, y_ref, gt_ref, gw_ref, gb_ref, wa_ref, ws_ref, o_ref):
    g = _gelu(y_ref[...])
    z = _bdot(g.astype(BF16), gw_ref[...]) + gb_ref[...]
    ssm = g * jax.nn.sigmoid(z)
    att = jnp.concatenate([att_ref[h] for h in range(MLA_HEADS)], axis=1)
    o = _bdot(att.astype(BF16), wa_ref[...]) + _bdot(ssm.astype(BF16), ws_ref[...])
    o_ref[...] = x_ref[...] + gt_ref[...] * o


def _even_out(x, att, y, gate, glu_w, glu_b, w_out):
    n, d = x.shape
    tm = min(512, n)
    wa = w_out[:MLA_HEADS * V_DIM].astype(BF16)
    ws = w_out[MLA_HEADS * V_DIM:].astype(BF16)
    gw = glu_w.astype(BF16)
    gb = glu_b.reshape(1, S5_WIDTH)
    full = lambda a: pl.BlockSpec(a.shape, lambda i: (0,) * a.ndim)
    return pl.pallas_call(
        _evout_kernel,
        grid=(n // tm,),
        in_specs=[pl.BlockSpec((tm, d), lambda i: (i, 0)),
                  pl.BlockSpec((MLA_HEADS, tm, V_DIM), lambda i: (0, i, 0)),
                  pl.BlockSpec((tm, S5_WIDTH), lambda i: (i, 0)),
                  full(gate), full(gw), full(gb), full(wa), full(ws)],
        out_specs=pl.BlockSpec((tm, d), lambda i: (i, 0)),
        out_shape=jax.ShapeDtypeStruct((n, d), F32),
        compiler_params=_cparams("parallel"),
        name="even_out",
    )(x, att, y, gate, gw, gb, wa, ws)


def _resid_kernel(x_ref, y_ref, gt_ref, w_ref, o_ref):
    o_ref[...] = x_ref[...] + gt_ref[...] * _bdot(y_ref[...].astype(BF16), w_ref[...])


def _resid_proj(x, y, gate, w):
    n, d = x.shape
    tm = min(512, n)
    wb = w.astype(BF16)
    full = lambda a: pl.BlockSpec(a.shape, lambda i: (0,) * a.ndim)
    return pl.pallas_call(
        _resid_kernel,
        grid=(n // tm,),
        in_specs=[pl.BlockSpec((tm, d), lambda i: (i, 0)), pl.BlockSpec((tm, y.shape[1]), lambda i: (i, 0)),
                  full(gate), full(wb)],
        out_specs=pl.BlockSpec((tm, d), lambda i: (i, 0)),
        out_shape=jax.ShapeDtypeStruct((n, d), F32),
        compiler_params=_cparams("parallel"),
        name="resid_proj",
    )(x, y, gate, wb)


def _sconv_kernel(z_ref, zp_ref, zn_ref, w_ref, b_ref, o_ref):
    i = pl.program_id(0)
    z = z_ref[...]
    tm = z.shape[0]
    rows = lax.broadcasted_iota(jnp.int32, z.shape, 0)
    prev_row = jnp.where(i > 0, zp_ref[SUBLANES - 1:SUBLANES, :], 0.0)
    next_row = jnp.where(i < pl.num_programs(0) - 1, zn_ref[0:1, :], 0.0)
    z_dn = jnp.where(rows == 0, prev_row, pltpu.roll(z, 1, 0))
    z_up = jnp.where(rows == tm - 1, next_row, pltpu.roll(z, tm - 1, 0))
    o_ref[...] = w_ref[0:1, :] * z_dn + w_ref[1:2, :] * z + w_ref[2:3, :] * z_up + b_ref[...]


def _short_conv(z, conv_w, conv_b):
    n, ch = z.shape
    tm = min(512, n)
    tc = 2048
    per = tm // SUBLANES
    last = n // SUBLANES - 1
    return pl.pallas_call(
        _sconv_kernel,
        grid=(n // tm, ch // tc),
        in_specs=[pl.BlockSpec((tm, tc), lambda i, j: (i, j)),
                  pl.BlockSpec((SUBLANES, tc), lambda i, j: (jnp.maximum(i * per - 1, 0), j)),
                  pl.BlockSpec((SUBLANES, tc), lambda i, j: (jnp.minimum((i + 1) * per, last), j)),
                  pl.BlockSpec((conv_w.shape[0], tc), lambda i, j: (0, j)),
                  pl.BlockSpec((1, tc), lambda i, j: (0, j))],
        out_specs=pl.BlockSpec((tm, tc), lambda i, j: (i, j)),
        out_shape=jax.ShapeDtypeStruct((n, ch), F32),
        compiler_params=_cparams("parallel", "parallel"),
        name="short_conv",
    )(z, z, z, conv_w, conv_b.reshape(1, ch))


def _filt_kernel(bands_ref, w1t_ref, w1c_ref, w1s_ref, b1_ref, w2_ref, b2_ref, w3_ref, b3_ref, fr_ref,
                 woh_ref, wol_ref, dl_ref, g_ref, s_ref, *, seq):
    i = pl.program_id(0)
    tl = g_ref.shape[1]
    rows = tl // FILT_PACK

    def lag_of(m):
        return jnp.where(m < seq, m, 2 * seq - m).astype(F32) / seq

    def packed_t(lanes_per_group):
        shape = (rows, FILT_PACK * lanes_per_group)
        r = lax.broadcasted_iota(jnp.int32, shape, 0)
        grp = lax.shift_right_logical(lax.broadcasted_iota(jnp.int32, shape, 1), int(math.log2(lanes_per_group)))
        return lag_of(i * tl + grp * rows + r)

    ang = ((2.0 * math.pi) * packed_t(HY_BANDS)) * bands_ref[...]
    fr = fr_ref[...]
    pre = (packed_t(HY_FILT_HIDDEN) * w1t_ref[...] + _hdot(jnp.cos(ang), w1c_ref[...])
           + _hdot(-jnp.sin(ang), w1s_ref[...]) + b1_ref[...])
    h = jnp.sin(fr * pre)
    h = jnp.sin(fr * (_hdot(h, w2_ref[...]) + b2_ref[...]))
    h = jnp.sin(fr * (_hdot(h, w3_ref[...]) + b3_ref[...]))
    lane = lax.broadcasted_iota(jnp.int32, (rows, LANES), 1)
    pieces = []
    for gi in range(FILT_PACK):
        slab = h[:, (gi // 2) * LANES:(gi // 2 + 1) * LANES]
        if gi % 2:
            slab = pltpu.roll(slab, HY_FILT_HIDDEN, 1)
        pieces.append(jnp.where(lane < HY_FILT_HIDDEN, slab, 0.0))
    hid = jnp.concatenate(pieces, axis=0)
    m = i * tl + lax.broadcasted_iota(jnp.int32, (tl, 1), 0)
    window = jnp.where(m == seq, 0.0, jnp.exp(-lag_of(m) * dl_ref[...]))

    @pl.when(i == 0)
    def _():
        s_ref[...] = jnp.zeros_like(s_ref)

    hid = _split(hid)
    for o in range(g_ref.shape[0]):
        f = _dot3(hid, (woh_ref[o], wol_ref[o])) * window
        g_ref[o] = f
        s_ref[o] += jnp.sum(jnp.abs(f), axis=0, keepdims=True)


def _hyena_filter(seq, w1, b1, w2, b2, w3, b3, freq, w_out):
    fh = HY_FILT_HIDDEN
    orders = w_out.shape[1]
    w = HY_WIDTH
    tl = min(512, seq)
    assert tl % (FILT_PACK * SUBLANES) == 0 and FILT_PACK * HY_BANDS == LANES
    eye = jnp.eye(FILT_PACK, dtype=F32)
    pad2 = lambda a: jnp.kron(eye, a)
    padv = lambda a: jnp.tile(a.reshape(1, -1), (1, FILT_PACK))
    bands = padv(jnp.linspace(1e-4, HY_BANDS - 1, HY_BANDS, dtype=F32))
    w1t = padv(w1[0])
    w1c = pad2(w1[1:1 + HY_BANDS])
    w1s = pad2(w1[1 + HY_BANDS:])
    wo_hi, wo_lo = _split(jnp.pad(w_out.transpose(1, 2, 0, 3),
                                  ((0, 0), (0, 0), (0, LANES - fh), (0, 0))))
    deltas = jnp.abs(jnp.linspace(HY_DECAY_MIN, HY_DECAY_MAX, w, dtype=F32)).reshape(1, w)
    nt = 2 * seq // tl
    half = seq // tl
    full = lambda a: pl.BlockSpec(a.shape, lambda i: (0,) * a.ndim)
    args = (bands, w1t, w1c, w1s, padv(b1), pad2(w2), padv(b2), pad2(w3), padv(b3), padv(freq))
    return pl.pallas_call(
        functools.partial(_filt_kernel, seq=seq),
        grid=(nt,),
        in_specs=[full(a) for a in args] + [pl.BlockSpec((orders, None, LANES, w), lambda i: (0, i // half, 0, 0)),
                                            pl.BlockSpec((orders, None, LANES, w), lambda i: (0, i // half, 0, 0)),
                                            full(deltas)],
        out_specs=[pl.BlockSpec((orders, tl, w), lambda i: (0, i, 0)),
                   pl.BlockSpec((orders, 1, w), lambda i: (0, 0, 0))],
        out_shape=[jax.ShapeDtypeStruct((orders, 2 * seq, w), F32), jax.ShapeDtypeStruct((orders, 1, w), F32)],
        compiler_params=_cparams("arbitrary"),
        name="hyena_filter",
    )(*args, wo_hi, wo_lo, deltas)


def _dft_consts(seq):
    n = 2 * seq
    n2 = DFT_N2
    n1 = n // n2
    k1 = n1 // 2 + 1
    two_pi = 2.0 * np.pi
    kk, nn = np.meshgrid(np.arange(k1), np.arange(n1), indexing="ij")
    ang_a = two_pi * ((kk * nn) % n1) / n1
    fa = (np.cos(ang_a), -np.sin(ang_a))
    k2, m2 = np.meshgrid(np.arange(n2), np.arange(n2), indexing="ij")
    ang_c = two_pi * ((k2 * m2) % n2) / n2
    fc = (np.cos(ang_c), -np.sin(ang_c))
    kk, mm = np.meshgrid(np.arange(k1), np.arange(n2), indexing="ij")
    ang_t = two_pi * ((kk * mm) % n) / n
    tw = (np.cos(ang_t), -np.sin(ang_t))
    weight = np.where((np.arange(k1) == 0) | (np.arange(k1) == n1 // 2), 1.0, 2.0) / n
    rows, cols = np.meshgrid(np.arange(n1 // 2), np.arange(k1), indexing="ij")
    ang_f = two_pi * ((rows * cols) % n1) / n1
    mc = np.cos(ang_f) * weight[None, :]
    ms = -np.sin(ang_f) * weight[None, :]
    kp = -(-k1 // SUBLANES) * SUBLANES
    rpad = lambda a: np.pad(a, ((0, kp - k1), (0, 0)))
    cpad = lambda a: np.pad(a, ((0, 0), (0, kp - k1)))
    f32 = lambda a: jnp.asarray(a, F32)
    fc3 = jnp.stack([_lhs3(f32(m)) for m in (fc[0], fc[1] - fc[0], fc[0] + fc[1])])
    mcs = jnp.concatenate([f32(cpad(mc)), f32(cpad(ms))], axis=1)
    return {"n1": n1, "k1": k1, "kp": kp,
            "fa_r": f32(rpad(fa[0])), "fa_i": f32(rpad(fa[1])), "fc3": fc3,
            "twa_r": f32(rpad(tw[0]).T[:, :, None]), "twa_i": f32(rpad(tw[1]).T[:, :, None]),
            "twc_r": f32(rpad(tw[0])[:, :, None]), "twc_i": f32(rpad(tw[1])[:, :, None]),
            "mcs3": _lhs3(mcs)}


def _dfta_kernel(x_ref, f_ref, twr_ref, twi_ref, or_ref, oi_ref):
    kp = or_ref.shape[0]
    f = f_ref[...]
    xt = pltpu.einshape("nsw->snw", x_ref[...])
    out_r, out_i = [], []
    for s in range(SUBLANES):
        a = _bdot(f, _rhs3(xt[s]))
        ar, ai = a[:kp], a[kp:]
        tr = twr_ref[s]
        ti = twi_ref[s]
        out_r.append(ar * tr - ai * ti)
        out_i.append(ar * ti + ai * tr)
    or_ref[...] = pltpu.einshape("skw->ksw", jnp.stack(out_r))
    oi_ref[...] = pltpu.einshape("skw->ksw", jnp.stack(out_i))


def _dft_stage_a(x3, col0, dc, n1c, row_blk=0):
    k1 = dc["kp"]
    tw = 1024
    wt = HY_WIDTH // tw
    f3 = _lhs3(jnp.concatenate([dc["fa_r"][:, :n1c], dc["fa_i"][:, :n1c]], axis=0))
    out = jax.ShapeDtypeStruct((k1, DFT_N2, HY_WIDTH), F32)
    return pl.pallas_call(
        _dfta_kernel,
        grid=(DFT_N2 // SUBLANES, wt),
        in_specs=[pl.BlockSpec((n1c, SUBLANES, tw), lambda b, j: (row_blk, b, col0 * wt + j)),
                  pl.BlockSpec(f3.shape, lambda b, j: (0, 0)),
                  pl.BlockSpec((SUBLANES, k1, 1), lambda b, j: (b, 0, 0)),
                  pl.BlockSpec((SUBLANES, k1, 1), lambda b, j: (b, 0, 0))],
        out_specs=[pl.BlockSpec((k1, SUBLANES, tw), lambda b, j: (0, b, j))] * 2,
        out_shape=[out, out],
        compiler_params=_cparams("parallel", "parallel"),
        name="dft_stage_a",
    )(x3, f3, dc["twa_r"], dc["twa_i"])


def _cmul3(f_ref, ar, ai):
    k1 = _bdot(f_ref[0], _rhs3(ar + ai))
    return k1 - _bdot(f_ref[2], _rhs3(ai)), k1 + _bdot(f_ref[1], _rhs3(ar))


def _dftc_filt_kernel(ar_ref, ai_ref, f_ref, s_ref, gr_ref, gi_ref, *, k1):
    @pl.when(pl.program_id(0) < k1)
    def _():
        xr, xi = _cmul3(f_ref, ar_ref[...], ai_ref[...])
        inv = 1.0 / s_ref[...]
        gr_ref[...] = xr * inv
        gi_ref[...] = xi * inv

    @pl.when(pl.program_id(0) >= k1)
    def _():
        gr_ref[...] = jnp.zeros_like(gr_ref)
        gi_ref[...] = jnp.zeros_like(gi_ref)


def _dft_filter_spectrum(ar, ai, abs_sum, dc):
    kp = dc["kp"]
    tw = 1024
    blk = pl.BlockSpec((None, DFT_N2, tw), lambda k, j: (k, 0, j))
    mat = pl.BlockSpec(dc["fc3"].shape, lambda k, j: (0, 0, 0))
    out = jax.ShapeDtypeStruct((kp, DFT_N2, HY_WIDTH), F32)
    return pl.pallas_call(
        functools.partial(_dftc_filt_kernel, k1=dc["k1"]),
        grid=(kp, HY_WIDTH // tw),
        in_specs=[blk, blk, mat, pl.BlockSpec((1, tw), lambda k, j: (0, j))],
        out_specs=[blk, blk],
        out_shape=[out, out],
        compiler_params=_cparams("parallel", "parallel"),
        name="dft_filter_spectrum",
    )(ar, ai, dc["fc3"], abs_sum)


def _dftmid_kernel(ar_ref, ai_ref, gr_ref, gi_ref, f_ref, twr_ref, twi_ref, or_ref, oi_ref, *, k1):
    @pl.when(pl.program_id(0) < k1)
    def _():
        xr, xi = _cmul3(f_ref, ar_ref[...], ai_ref[...])
        gr = gr_ref[...]
        gi = gi_ref[...]
        yr = xr * gr - xi * gi
        yi = xr * gi + xi * gr
        k1_ = _bdot(f_ref[0], _rhs3(yr + yi))
        br = k1_ + _bdot(f_ref[1], _rhs3(yi))
        bi = k1_ - _bdot(f_ref[2], _rhs3(yr))
        tr = twr_ref[...]
        ti = twi_ref[...]
        or_ref[...] = br * tr + bi * ti
        oi_ref[...] = bi * tr - br * ti

    @pl.when(pl.program_id(0) >= k1)
    def _():
        or_ref[...] = jnp.zeros_like(or_ref)
        oi_ref[...] = jnp.zeros_like(oi_ref)


def _dft_middle(ar, ai, gr, gi, dc):
    kp = dc["kp"]
    tw = 1024
    blk = pl.BlockSpec((None, DFT_N2, tw), lambda k, j: (k, 0, j))
    mat = pl.BlockSpec(dc["fc3"].shape, lambda k, j: (0, 0, 0))
    twd = pl.BlockSpec((None, DFT_N2, 1), lambda k, j: (k, 0, 0))
    out = jax.ShapeDtypeStruct((kp, DFT_N2, HY_WIDTH), F32)
    return pl.pallas_call(
        functools.partial(_dftmid_kernel, k1=dc["k1"]),
        grid=(kp, HY_WIDTH // tw),
        in_specs=[blk, blk, blk, blk, mat, twd, twd],
        out_specs=[blk, blk],
        out_shape=[out, out],
        compiler_params=_cparams("parallel", "parallel"),
        name="dft_middle",
    )(ar, ai, gr, gi, dc["fc3"], dc["twc_r"], dc["twc_i"])


def _dftf_kernel(br_ref, bi_ref, m_ref, gate_ref, y_ref, skip_ref, o_ref):
    m = m_ref[...]
    brt = pltpu.einshape("ksw->skw", br_ref[...])
    bit = pltpu.einshape("ksw->skw", bi_ref[...])
    conv = [_bdot(m, _rhs3(jnp.concatenate([brt[s], bit[s]], axis=0))) for s in range(SUBLANES)]
    conv = pltpu.einshape("snw->nsw", jnp.stack(conv))
    o_ref[...] = gate_ref[...] * (conv + skip_ref[...] * y_ref[...])


def _dft_final(br, bi, dc, gate3, gate_col, y3, y_col, skip):
    k1 = dc["kp"]
    nh = dc["n1"] // 2
    tw = 1024
    wt = HY_WIDTH // tw
    bblk = pl.BlockSpec((k1, SUBLANES, tw), lambda b, j: (0, b, j))
    mat = pl.BlockSpec(dc["mcs3"].shape, lambda b, j: (0, 0))
    return pl.pallas_call(
        _dftf_kernel,
        grid=(DFT_N2 // SUBLANES, wt),
        in_specs=[bblk, bblk, mat,
                  pl.BlockSpec((nh, SUBLANES, tw), lambda b, j: (0, b, gate_col * wt + j)),
                  pl.BlockSpec((nh, SUBLANES, tw), lambda b, j: (0, b, y_col * wt + j)),
                  pl.BlockSpec((1, tw), lambda b, j: (0, j))],
        out_specs=pl.BlockSpec((nh, SUBLANES, tw), lambda b, j: (0, b, j)),
        out_shape=jax.ShapeDtypeStruct((nh, DFT_N2, HY_WIDTH), F32),
        compiler_params=_cparams("parallel", "parallel"),
        name="dft_final",
    )(br, bi, dc["mcs3"], gate3, y3, skip)


def _hyena_mixer(x, shift, scale, gate, g_norm, w_in, conv_w, conv_b, w1, b1, w2, b2, w3, b3, freq,
                 filt_w_out, skip, w_out):
    seq = x.shape[0]
    dc = _dft_consts(seq)
    nh = dc["n1"] // 2
    z = _norm_proj(x, shift, scale, g_norm, w_in)
    zc = _short_conv(z, conv_w, conv_b)
    zc3 = zc.reshape(nh, DFT_N2, 3 * HY_WIDTH)
    y3, y_col = zc3, 2
    g_all, abs_sums = _hyena_filter(seq, w1, b1, w2, b2, w3, b3, freq, filt_w_out)
    for o in range(filt_w_out.shape[1]):
        ga_r, ga_i = _dft_stage_a(g_all.reshape(-1, DFT_N2, HY_WIDTH), 0, dc, dc["n1"], row_blk=o)
        gr, gi = _dft_filter_spectrum(ga_r, ga_i, abs_sums[o], dc)
        ya_r, ya_i = _dft_stage_a(y3, y_col, dc, nh)
        br, bi = _dft_middle(ya_r, ya_i, gr, gi, dc)
        y3 = _dft_final(br, bi, dc, zc3, o, y3, y_col, skip[o].reshape(1, HY_WIDTH))
        y_col = 0
    return _resid_proj(x, y3.reshape(seq, HY_WIDTH), gate, w_out)


def _even_mixer(x, cx, mx, mc, g_norm, ev_w_in, q_a_norm_g, w_uq, kv_a_norm_g, w_ukv, q_head_g, k_head_g,
                lam_re, lam_im, log_dt, b_re, b_im, c_re, c_im, d_skip, glu_w, glu_b, ev_w_out):
    n = x.shape[0]
    nc = cx.shape[0]
    cuts = (Q_LORA, Q_LORA + KV_LORA, Q_LORA + KV_LORA + QK_ROPE)
    w_qa, w_kva, w_kpe, w_u = (ev_w_in[:, :cuts[0]], ev_w_in[:, cuts[0]:cuts[1]],
                               ev_w_in[:, cuts[1]:cuts[2]], ev_w_in[:, cuts[2]:])
    w_all = jnp.concatenate([w_u, w_qa, w_kva, _pad_lanes(w_kpe), _pad_lanes(w_kpe[:, _ROPE_SWAP])], axis=1)
    proj_x = _norm_proj(x, mx[3], mx[4], g_norm, w_all)
    proj_c = _norm_proj(cx, mc[3], mc[4], g_norm, w_all)

    wq3, wkv, gq3, gk3 = _mla_weights(w_uq, w_ukv, q_head_g, k_head_g)
    gqa = q_a_norm_g.reshape(1, Q_LORA)
    gkva = kv_a_norm_g.reshape(1, KV_LORA)
    cos_x, sin_x = _rope_tables(n)
    cos_c = _pad_lanes(jnp.ones((nc, QK_ROPE), F32))
    sin_c = jnp.zeros((nc, LANES), F32)
    q_x, k_x, v_x = _mla_heads(proj_x, cos_x, sin_x, gqa, gkva, wq3, wkv, gq3, gk3)
    _, k_c, v_c = _mla_heads(proj_c, cos_c, sin_c, gqa, gkva, wq3, wkv, gq3, gk3)
    att = _attention(q_x, jnp.concatenate([k_x, k_c], axis=1), jnp.concatenate([v_x, v_c], axis=1))

    d_row = d_skip.reshape(1, S5_WIDTH)
    zero = jnp.zeros((SUBLANES, S5_NSTATE), F32)
    y = None
    for direction, reverse in ((0, False), (1, True)):
        prm = _s5_params(lam_re[direction], lam_im[direction], log_dt[direction], b_re[direction],
                         b_im[direction], c_re[direction], c_im[direction], reverse)
        _, hc_r, hc_i = _s5_scan(proj_c, d_row, (zero, zero), prm, reverse=reverse, first=True)
        if y is None:
            y, _, _ = _s5_scan(proj_x, d_row, (hc_r, hc_i), prm, reverse=reverse, first=True)
        else:
            y, _, _ = _s5_scan(proj_x, y, (hc_r, hc_i), prm, reverse=reverse, first=False)
    return _even_out(x, att, y, mx[5], glu_w, glu_b, ev_w_out)


def kernel(x, c, ctx, c_ctx, ada_w, ada_b, norm_g, ffn_w_in, ffn_w_out, ev_w_in, mla_q_a_norm_g, mla_w_uq, mla_kv_a_norm_g, mla_w_ukv, mla_q_head_g, mla_k_head_g, s5_lam_re, s5_lam_im, s5_log_dt, s5_b_re, s5_b_im, s5_c_re, s5_c_im, s5_d, s5_glu_w, s5_glu_b, ev_w_out, hy_w_in, hy_conv_w, hy_conv_b, hy_filt_w1, hy_filt_b1, hy_filt_w2, hy_filt_b2, hy_filt_w3, hy_filt_b3, hy_filt_freq, hy_filt_w_out, hy_skip, hy_w_out):
    assert x.shape[0] == 1 and ctx.shape[0] == 1, "kernels are written for batch 1"
    depth = ada_w.shape[0]
    mods = _ada_mods(c, c_ctx, ada_w, ada_b)
    xs = x[0]
    cx = ctx[0]
    for i in range(depth):
        even = i % 2 == 0
        li = i // 2
        need_after = any(j % 2 == 0 for j in range(i + 1, depth))
        use_ctx = even or need_after
        assert not need_after, "context outputs after an even layer are not needed at this depth"
        mx = mods[i, 0]
        mc = mods[i, 1]
        xs = _ffn_half(xs, mx, 0, norm_g[i, 0], ffn_w_in[i, 0], ffn_w_out[i, 0])
        if use_ctx:
            cx = _ffn_half(cx, mc, 0, norm_g[i, 0], ffn_w_in[i, 0], ffn_w_out[i, 0])
        if even:
            xs = _even_mixer(xs, cx, mx, mc, norm_g[i, 1], ev_w_in[li], mla_q_a_norm_g[li], mla_w_uq[li],
                             mla_kv_a_norm_g[li], mla_w_ukv[li], mla_q_head_g[li], mla_k_head_g[li],
                             s5_lam_re[li], s5_lam_im[li], s5_log_dt[li], s5_b_re[li], s5_b_im[li],
                             s5_c_re[li], s5_c_im[li], s5_d[li], s5_glu_w[li], s5_glu_b[li], ev_w_out[li])
        else:
            xs = _hyena_mixer(xs, mx[3], mx[4], mx[5], norm_g[i, 1], hy_w_in[li], hy_conv_w[li], hy_conv_b[li],
                              hy_filt_w1[li], hy_filt_b1[li], hy_filt_w2[li], hy_filt_b2[li], hy_filt_w3[li],
                              hy_filt_b3[li], hy_filt_freq[li], hy_filt_w_out[li], hy_skip[li], hy_w_out[li])
        xs = _ffn_half(xs, mx, 6, norm_g[i, 2], ffn_w_in[i, 1], ffn_w_out[i, 1])
    return xs[None]
```

```python
import functools
import math

import numpy as np
import jax
import jax.numpy as jnp
from jax import lax
from jax.experimental import pallas as pl
from jax.experimental.pallas import tpu as pltpu

F32 = jnp.float32
BF16 = jnp.bfloat16
HIGHEST = lax.Precision.HIGHEST

D_MODEL = 2048
N_MOD = 9
D_FF = 5632
NORM_EPS = 1e-6
GRID_W = 64
MLA_HEADS = 8
QK_NOPE = 128
QK_ROPE = 64
QK_DIM = QK_NOPE + QK_ROPE
V_DIM = 128
Q_LORA = 512
KV_LORA = 256
ROPE_BASE = 10000.0
S5_WIDTH = 1024
S5_GROUP = 16
S5_GROUPS = S5_WIDTH // S5_GROUP
S5_STATE = 64
S5_NSTATE = S5_GROUPS * S5_STATE
S5_CHUNKS = 4
HY_WIDTH = D_MODEL
HY_BANDS = 16
HY_FILT_HIDDEN = 64
HY_DECAY_MIN = math.log(1e-2) / 1.5
HY_DECAY_MAX = math.log(1e-2) / 0.3
DFT_N2 = 256
FILT_PACK = 8
LANES = 128
SUBLANES = 8
VMEM_LIMIT = 56 * 1024 * 1024


def _cparams(*sem):
    return pltpu.CompilerParams(dimension_semantics=sem, vmem_limit_bytes=VMEM_LIMIT)


def _hdot(a, b):
    return jnp.dot(a, b, preferred_element_type=F32, precision=HIGHEST)


def _bdot(a, b):
    return jnp.dot(a, b, preferred_element_type=F32)


def _split(x):
    hi = x.astype(BF16)
    return hi, (x - hi.astype(F32)).astype(BF16)


def _dot3(a, b):
    return _bdot(a[0], b[0]) + (_bdot(a[1], b[0]) + _bdot(a[0], b[1]))


def _lhs3(a):
    hi, lo = _split(a)
    return jnp.concatenate([hi, lo, hi], axis=1)


def _rhs3(b):
    hi, lo = _split(b)
    return jnp.concatenate([hi, hi, lo], axis=0)


def _norm_mod(x, g, scale, shift):
    y = x * lax.rsqrt(jnp.mean(x * x, axis=-1, keepdims=True) + NORM_EPS) * g
    return y * (1.0 + scale) + shift


def _mods_kernel(s_ref, w_ref, b_ref, o_ref):
    s = s_ref[...]
    s = s * jax.nn.sigmoid(s)
    o_ref[...] = _hdot(s, w_ref[...]) + b_ref[...]


def _ada_mods(c, c_ctx, ada_w, ada_b):
    depth, d, nd = ada_w.shape
    s = jnp.zeros((SUBLANES, d), F32).at[0].set(c[0]).at[1].set(c_ctx)
    tn = 1024
    out = pl.pallas_call(
        _mods_kernel,
        grid=(depth, nd // tn),
        in_specs=[pl.BlockSpec((SUBLANES, d), lambda l, j: (0, 0)),
                  pl.BlockSpec((None, d, tn), lambda l, j: (l, 0, j)),
                  pl.BlockSpec((None, 1, tn), lambda l, j: (l, 0, j))],
        out_specs=pl.BlockSpec((None, SUBLANES, tn), lambda l, j: (l, 0, j)),
        out_shape=jax.ShapeDtypeStruct((depth, SUBLANES, nd), F32),
        compiler_params=_cparams("parallel", "parallel"),
        name="ada_mods",
    )(s, ada_w, ada_b.reshape(depth, 1, nd))
    return out[:, :2].reshape(depth, 2, N_MOD, 1, d)


FFN_NORM_SLICES = 8


def _ffn_kernel(x_ref, xn_ref, sh_ref, sc_ref, gt_ref, g_ref, wg_ref, wu_ref, wo_ref, o_ref,
                h0_scr, h1_scr, acc_scr):
    i = pl.program_id(0)
    j = pl.program_id(1)
    rows = xn_ref.shape[0]

    def norm(x):
        return _norm_mod(x, g_ref[...], sc_ref[...], sh_ref[...]).astype(BF16)

    @pl.when((i == 0) & (j == 0))
    def _():
        h0_scr[...] = norm(x_ref[...])

    @pl.when(j == 0)
    def _():
        acc_scr[...] = jnp.zeros_like(acc_scr)

    def step(h_cur, h_nxt):
        r0 = pl.multiple_of(jnp.minimum(j, FFN_NORM_SLICES - 1) * rows, rows)
        h_nxt[pl.ds(r0, rows), :] = norm(xn_ref[...])
        h = h_cur[...]
        g = _bdot(h, wg_ref[...])
        u = _bdot(h, wu_ref[...])
        a = (g * jax.nn.sigmoid(g)) * u
        acc_scr[...] += _bdot(a.astype(BF16), wo_ref[...])

    @pl.when(lax.rem(i, 2) == 0)
    def _():
        step(h0_scr, h1_scr)

    @pl.when(lax.rem(i, 2) == 1)
    def _():
        step(h1_scr, h0_scr)

    @pl.when(j == pl.num_programs(1) - 1)
    def _():
        o_ref[...] = x_ref[...] + gt_ref[...] * (0.5 * acc_scr[...])


def _ffn_half(x, mod, k0, g_norm, w_in, w_out):
    n, d = x.shape
    f = w_out.shape[0]
    tm = min(512, n)
    tf = 512
    nf = f // tf
    nt = n // tm
    assert nf >= FFN_NORM_SLICES and tm % (FFN_NORM_SLICES * 2 * SUBLANES) == 0
    vec = pl.BlockSpec((1, d), lambda i, j: (0, 0))
    rows = tm // FFN_NORM_SLICES

    def next_slice(i, j):
        return jnp.minimum(i + 1, nt - 1) * FFN_NORM_SLICES + jnp.minimum(j, FFN_NORM_SLICES - 1), 0

    return pl.pallas_call(
        _ffn_kernel,
        grid=(nt, nf),
        in_specs=[pl.BlockSpec((tm, d), lambda i, j: (i, 0)),
                  pl.BlockSpec((rows, d), next_slice),
                  vec, vec, vec, vec,
                  pl.BlockSpec((d, tf), lambda i, j: (0, j)),
                  pl.BlockSpec((d, tf), lambda i, j: (0, j + nf)),
                  pl.BlockSpec((tf, d), lambda i, j: (j, 0))],
        out_specs=pl.BlockSpec((tm, d), lambda i, j: (i, 0)),
        out_shape=jax.ShapeDtypeStruct((n, d), F32),
        scratch_shapes=[pltpu.VMEM((tm, d), BF16), pltpu.VMEM((tm, d), BF16), pltpu.VMEM((tm, d), F32)],
        compiler_params=_cparams("arbitrary", "arbitrary"),
        name="ffn_half",
    )(x, x, mod[k0], mod[k0 + 1], mod[k0 + 2], g_norm.reshape(1, d),
      w_in.astype(BF16), w_in.astype(BF16), w_out.astype(BF16))


def _proj_kernel(x_ref, sh_ref, sc_ref, g_ref, w_ref, o_ref, h_scr):
    @pl.when(pl.program_id(1) == 0)
    def _():
        h_scr[...] = _norm_mod(x_ref[...], g_ref[...], sc_ref[...], sh_ref[...]).astype(BF16)

    o_ref[...] = _bdot(h_scr[...], w_ref[...])


def _norm_proj(x, shift, scale, g_norm, w):
    n, d = x.shape
    nout = w.shape[1]
    tm = min(512, n)
    tn = 2048
    vec = pl.BlockSpec((1, d), lambda i, j: (0, 0))
    return pl.pallas_call(
        _proj_kernel,
        grid=(n // tm, nout // tn),
        in_specs=[pl.BlockSpec((tm, d), lambda i, j: (i, 0)), vec, vec, vec,
                  pl.BlockSpec((d, tn), lambda i, j: (0, j))],
        out_specs=pl.BlockSpec((tm, tn), lambda i, j: (i, j)),
        out_shape=jax.ShapeDtypeStruct((n, nout), F32),
        scratch_shapes=[pltpu.VMEM((tm, d), BF16)],
        compiler_params=_cparams("parallel", "arbitrary"),
        name="norm_proj",
    )(x, shift, scale, g_norm.reshape(1, d), w.astype(BF16))


def _mla_kernel(qa_ref, kva_ref, kpe_ref, cos_ref, sin_ref, gqa_ref, gkva_ref, wq_ref, wkv_ref,
                gq_ref, gk_ref, q_ref, k_ref, v_ref):
    def rms(t, g):
        return t * lax.rsqrt(jnp.mean(t * t, axis=-1, keepdims=True) + NORM_EPS) * g

    qn = rms(qa_ref[...], gqa_ref[...]).astype(BF16)
    kvn = rms(kva_ref[...], gkva_ref[...]).astype(BF16)
    cos = cos_ref[...]
    sin = sin_ref[...]
    gq = gq_ref[...]
    gk = gk_ref[...]
    kpe = kpe_ref[:, :LANES]
    kpe_sw = kpe_ref[:, LANES:]
    kpe_ss = jnp.sum(kpe * kpe, axis=-1, keepdims=True)
    k_rot = (kpe * gk[:, LANES:2 * LANES]) * cos + (kpe_sw * gk[:, 2 * LANES:]) * sin
    inv_dim = 1.0 / QK_DIM
    q_scale = math.log2(math.e) / math.sqrt(QK_DIM)
    for h in range(MLA_HEADS):
        qh = _bdot(qn, wq_ref[h])
        nope, pe, pe_sw = qh[:, :LANES], qh[:, LANES:2 * LANES], qh[:, 2 * LANES:]
        ss = jnp.sum(nope * nope, axis=-1, keepdims=True) + jnp.sum(pe * pe, axis=-1, keepdims=True)
        r = lax.rsqrt(ss * inv_dim + NORM_EPS) * q_scale
        q_rot = (pe * gq[:, LANES:2 * LANES]) * cos + (pe_sw * gq[:, 2 * LANES:]) * sin
        q_ref[h, :, :LANES] = (nope * r * gq[:, :LANES]).astype(BF16)
        q_ref[h, :, LANES:] = (q_rot * r).astype(BF16)
        kv = _bdot(kvn, wkv_ref[h])
        k_nope, vv = kv[:, :LANES], kv[:, LANES:]
        ssk = jnp.sum(k_nope * k_nope, axis=-1, keepdims=True) + kpe_ss
        rk = lax.rsqrt(ssk * inv_dim + NORM_EPS)
        k_ref[h, :, :LANES] = (k_nope * rk * gk[:, :LANES]).astype(BF16)
        k_ref[h, :, LANES:] = (k_rot * rk).astype(BF16)
        v_ref[h] = vv.astype(BF16)


_ROPE_SWAP = np.concatenate([np.arange(16, 32), np.arange(0, 16), np.arange(48, 64), np.arange(32, 48)])


def _pad_lanes(a, width=LANES):
    return jnp.pad(a, [(0, 0)] * (a.ndim - 1) + [(0, width - a.shape[-1])])


def _mla_weights(w_uq, w_ukv, q_head_g, k_head_g):
    wq = w_uq.reshape(Q_LORA, MLA_HEADS, QK_DIM).transpose(1, 0, 2)
    wq_pe = wq[:, :, QK_NOPE:]
    wq3 = jnp.concatenate([wq[:, :, :QK_NOPE], _pad_lanes(wq_pe), _pad_lanes(wq_pe[:, :, _ROPE_SWAP])], axis=-1)
    wkv = w_ukv.reshape(KV_LORA, MLA_HEADS, QK_NOPE + V_DIM).transpose(1, 0, 2)

    def g3(g):
        pe = g[QK_NOPE:]
        return jnp.concatenate([g[:QK_NOPE], _pad_lanes(pe), _pad_lanes(pe[_ROPE_SWAP])]).reshape(1, 3 * LANES)

    return wq3.astype(BF16), wkv.astype(BF16), g3(q_head_g), g3(k_head_g)


def _rope_tables(n):
    row = jnp.repeat(jnp.arange(n // GRID_W, dtype=F32), GRID_W)
    col = jnp.tile(jnp.arange(GRID_W, dtype=F32), n // GRID_W)
    n_freq = QK_ROPE // 4
    inv_freq = ROPE_BASE ** (-jnp.arange(n_freq, dtype=F32) / n_freq)
    ang_r = row[:, None] * inv_freq
    ang_c = col[:, None] * inv_freq
    cr, sr, cc, sc = jnp.cos(ang_r), jnp.sin(ang_r), jnp.cos(ang_c), jnp.sin(ang_c)
    cos = jnp.concatenate([cr, cr, cc, cc], axis=-1)
    sin = jnp.concatenate([-sr, sr, -sc, sc], axis=-1)
    return _pad_lanes(cos), _pad_lanes(sin)


def _mla_heads(proj, cos, sin, gqa, gkva, wq3, wkv, gq3, gk3):
    n = proj.shape[0]
    tm = min(512, n)
    hd = 2 * LANES
    full = lambda a: pl.BlockSpec(a.shape, lambda i: (0,) * a.ndim)
    return pl.pallas_call(
        _mla_kernel,
        grid=(n // tm,),
        in_specs=[pl.BlockSpec((tm, Q_LORA), lambda i: (i, S5_WIDTH // Q_LORA)),
                  pl.BlockSpec((tm, KV_LORA), lambda i: (i, (S5_WIDTH + Q_LORA) // KV_LORA)),
                  pl.BlockSpec((tm, 2 * LANES), lambda i: (i, (S5_WIDTH + Q_LORA + KV_LORA) // (2 * LANES))),
                  pl.BlockSpec((tm, LANES), lambda i: (i, 0)),
                  pl.BlockSpec((tm, LANES), lambda i: (i, 0)),
                  full(gqa), full(gkva), full(wq3), full(wkv), full(gq3), full(gk3)],
        out_specs=[pl.BlockSpec((MLA_HEADS, tm, hd), lambda i: (0, i, 0)),
                   pl.BlockSpec((MLA_HEADS, tm, hd), lambda i: (0, i, 0)),
                   pl.BlockSpec((MLA_HEADS, tm, V_DIM), lambda i: (0, i, 0))],
        out_shape=[jax.ShapeDtypeStruct((MLA_HEADS, n, hd), BF16),
                   jax.ShapeDtypeStruct((MLA_HEADS, n, hd), BF16),
                   jax.ShapeDtypeStruct((MLA_HEADS, n, V_DIM), BF16)],
        compiler_params=_cparams("parallel"),
        name="mla_heads",
    )(proj, proj, proj, cos, sin, gqa, gkva, wq3, wkv, gq3, gk3)


def _attn_kernel(q_ref, k_ref, v_ref, o_ref, s_scr, bm_scr, m_scr, acc_scr, *, nkv):
    t = pl.program_id(0)
    tk = k_ref.shape[0]
    jp = lax.rem(t + (nkv - 1), nkv)

    @pl.when(t == 0)
    def _():
        s_scr[1] = jnp.zeros(s_scr.shape[1:], F32)
        bm_scr[1] = jnp.zeros(bm_scr.shape[1:], F32)
        m_scr[...] = jnp.zeros_like(m_scr)
        acc_scr[...] = jnp.zeros_like(acc_scr)

    def step(cur, prev):
        s_new = lax.dot_general(q_ref[...], k_ref[...], (((1,), (1,)), ((), ())), preferred_element_type=F32)
        s_scr[cur] = s_new
        bm_scr[cur] = jnp.max(s_new, axis=-1, keepdims=True)
        m_prev = jnp.where(jp == 0, -jnp.inf, m_scr[...])
        m_new = jnp.maximum(m_prev, bm_scr[prev])
        alpha = jnp.exp2(m_prev - m_new)
        p = jnp.exp2(s_scr[prev] - m_new).astype(BF16)
        v_ext = jnp.concatenate([v_ref[...], jnp.ones((tk, LANES), BF16)], axis=1)
        acc_scr[...] = alpha * acc_scr[...] + _bdot(p, v_ext)
        m_scr[...] = m_new

    @pl.when(lax.rem(t, 2) == 0)
    def _():
        step(0, 1)

    @pl.when(lax.rem(t, 2) == 1)
    def _():
        step(1, 0)

    @pl.when((jp == nkv - 1) & (t > 0))
    def _():
        acc = acc_scr[...]
        o_ref[...] = acc[:, :V_DIM] / acc[:, V_DIM:]


def _kv_tile(m):
    for t in (3328, 1280, 1024, 768, 512, 256, 128):
        if m % t == 0:
            return t
    raise ValueError(f"key length {m} is not a multiple of {LANES}")


def _attention(q, k, v):
    h, n, hd = q.shape
    m = k.shape[1]
    tq = min(1024, n)
    tk = _kv_tile(m)
    nq = n // tq
    nkv = m // tk
    total = h * nq * nkv

    def qk_idx(t):
        tt = jnp.minimum(t, total - 1)
        return tt // (nq * nkv), (tt // nkv) % nq, tt % nkv

    def pv_idx(t):
        tt = jnp.maximum(t - 1, 0)
        return tt // (nq * nkv), (tt // nkv) % nq, tt % nkv

    return pl.pallas_call(
        functools.partial(_attn_kernel, nkv=nkv),
        grid=(total + 1,),
        in_specs=[pl.BlockSpec((None, tq, hd), lambda t: (qk_idx(t)[0], qk_idx(t)[1], 0)),
                  pl.BlockSpec((None, tk, hd), lambda t: (qk_idx(t)[0], qk_idx(t)[2], 0)),
                  pl.BlockSpec((None, tk, V_DIM), lambda t: (pv_idx(t)[0], pv_idx(t)[2], 0))],
        out_specs=pl.BlockSpec((None, tq, V_DIM), lambda t: (pv_idx(t)[0], pv_idx(t)[1], 0)),
        out_shape=jax.ShapeDtypeStruct((h, n, V_DIM), F32),
        scratch_shapes=[pltpu.VMEM((2, tq, tk), F32), pltpu.VMEM((2, tq, 1), F32),
                        pltpu.VMEM((tq, 1), F32), pltpu.VMEM((tq, V_DIM + LANES), F32)],
        compiler_params=_cparams("arbitrary"),
        name="flash_attention",
    )(q, k, v)


S5_TILE = 256
S5_LANE_CHUNK = 512


def _s5_kernel(*refs, reverse, first):
    if first:
        (u_ref, d_ref, h0r_ref, h0i_ref, br_ref, bi_ref, cr_ref, ci_ref, tab_ref,
         y_ref, hr_ref, hi_ref, xr_scr, xi_scr) = refs
    else:
        (u_ref, yin_ref, h0r_ref, h0i_ref, br_ref, bi_ref, cr_ref, ci_ref, tab_ref,
         y_ref, hr_ref, hi_ref, xr_scr, xi_scr) = refs
    tm = u_ref.shape[0]
    nblk = tm // SUBLANES
    cin = S5_WIDTH // S5_CHUNKS
    cst = S5_NSTATE // S5_CHUNKS

    @pl.when(pl.program_id(0) == 0)
    def _():
        hr_ref[...] = h0r_ref[...]
        hi_ref[...] = h0i_ref[...]

    for q in range(S5_CHUNKS):
        uq = u_ref[:, q * cin:(q + 1) * cin].astype(BF16)
        xr_scr[:, q * cst:(q + 1) * cst] = _bdot(uq, br_ref[q])
        xi_scr[:, q * cst:(q + 1) * cst] = _bdot(uq, bi_ref[q])

    edge = 0 if reverse else SUBLANES - 1
    for jc in range(S5_NSTATE // S5_LANE_CHUNK):
        sl = slice(jc * S5_LANE_CHUNK, (jc + 1) * S5_LANE_CHUNK)

        def body(r, carry, sl=sl):
            car, cai = carry
            blk = (nblk - 1 - r) if reverse else r
            row = pl.multiple_of(blk * SUBLANES, SUBLANES)
            xr = xr_scr[pl.ds(row, SUBLANES), sl]
            xi = xi_scr[pl.ds(row, SUBLANES), sl]
            for idx, dist in enumerate((1, 2, 4)):
                ar = tab_ref[2 * idx, :, sl]
                ai = tab_ref[2 * idx + 1, :, sl]
                shift = SUBLANES - dist if reverse else dist
                sr = pltpu.roll(xr, shift, 0)
                si = pltpu.roll(xi, shift, 0)
                xr, xi = xr + ar * sr - ai * si, xi + ar * si + ai * sr
            pr = tab_ref[6, :, sl]
            pi = tab_ref[7, :, sl]
            xr, xi = xr + pr * car - pi * cai, xi + pr * cai + pi * car
            xr_scr[pl.ds(row, SUBLANES), sl] = xr
            xi_scr[pl.ds(row, SUBLANES), sl] = xi
            shape = (SUBLANES, S5_LANE_CHUNK)
            return (jnp.broadcast_to(xr[edge:edge + 1, :], shape), jnp.broadcast_to(xi[edge:edge + 1, :], shape))

        car, cai = lax.fori_loop(0, nblk, body, (hr_ref[:, sl], hi_ref[:, sl]), unroll=True)
        hr_ref[:, sl] = car
        hi_ref[:, sl] = cai

    cout = S5_WIDTH // S5_CHUNKS
    for q in range(S5_CHUNKS):
        hr = xr_scr[:, q * cst:(q + 1) * cst].astype(BF16)
        hi = xi_scr[:, q * cst:(q + 1) * cst].astype(BF16)
        y = _bdot(hr, cr_ref[q]) + _bdot(hi, ci_ref[q])
        cs = slice(q * cout, (q + 1) * cout)
        if first:
            base = u_ref[:, cs] * d_ref[:, cs]
        else:
            base = yin_ref[:, cs]
        y_ref[:, cs] = base + y


def _s5_scan(u_src, extra, h0, prm, *, reverse, first):
    n = u_src.shape[0]
    tm = min(S5_TILE, n)
    nt = n // tm
    tmap = (lambda i: (nt - 1 - i, 0)) if reverse else (lambda i: (i, 0))
    full = lambda a: pl.BlockSpec(a.shape, lambda i: (0,) * a.ndim)
    extra_spec = full(extra) if first else pl.BlockSpec((tm, S5_WIDTH), tmap)
    st = jax.ShapeDtypeStruct((SUBLANES, S5_NSTATE), F32)
    st_spec = pl.BlockSpec((SUBLANES, S5_NSTATE), lambda i: (0, 0))
    return pl.pallas_call(
        functools.partial(_s5_kernel, reverse=reverse, first=first),
        grid=(nt,),
        in_specs=[pl.BlockSpec((tm, S5_WIDTH), tmap), extra_spec, st_spec, st_spec,
                  full(prm["br"]), full(prm["bi"]), full(prm["cr"]), full(prm["ci"]), full(prm["tab"])],
        out_specs=[pl.BlockSpec((tm, S5_WIDTH), tmap), st_spec, st_spec],
        out_shape=[jax.ShapeDtypeStruct((n, S5_WIDTH), F32), st, st],
        scratch_shapes=[pltpu.VMEM((tm, S5_NSTATE), F32), pltpu.VMEM((tm, S5_NSTATE), F32)],
        compiler_params=_cparams("arbitrary"),
        name="s5_scan_rev" if reverse else "s5_scan_fwd",
    )(u_src, extra, h0[0], h0[1], prm["br"], prm["bi"], prm["cr"], prm["ci"], prm["tab"])


def _s5_params(lam_re, lam_im, log_dt, b_re, b_im, c_re, c_im, reverse):
    dt = jnp.exp(log_dt)[:, None]
    mag = jnp.exp(lam_re * dt)
    a_re = mag * jnp.cos(lam_im * dt)
    a_im = mag * jnp.sin(lam_im * dt)
    den = lam_re * lam_re + lam_im * lam_im
    nr = a_re - 1.0
    k_re = (nr * lam_re + a_im * lam_im) / den
    k_im = (a_im * lam_re - nr * lam_im) / den
    bx_re = k_re[:, :, None] * b_re - k_im[:, :, None] * b_im
    bx_im = k_re[:, :, None] * b_im + k_im[:, :, None] * b_re
    gpc = S5_GROUPS // S5_CHUNKS
    eye = jnp.eye(gpc, dtype=F32)

    def bdiag_in(b):
        b = b.reshape(S5_CHUNKS, gpc, S5_STATE, S5_GROUP)
        return jnp.einsum('qgpc,gh->qgchp', b, eye).reshape(S5_CHUNKS, gpc * S5_GROUP, gpc * S5_STATE)

    def bdiag_out(c):
        c = c.reshape(S5_CHUNKS, gpc, S5_GROUP, S5_STATE)
        return jnp.einsum('qgcp,gh->qgphc', c, eye).reshape(S5_CHUNKS, gpc * S5_STATE, gpc * S5_GROUP)

    pr, pi = [a_re.reshape(-1)], [a_im.reshape(-1)]
    for _ in range(SUBLANES - 1):
        pr, pi = pr + [pr[-1] * pr[0] - pi[-1] * pi[0]], pi + [pr[-1] * pi[0] + pi[-1] * pr[0]]
    t = np.arange(SUBLANES)
    tabs = []
    for dist in (1, 2, 4):
        keep = (t <= SUBLANES - 1 - dist) if reverse else (t >= dist)
        mask = jnp.asarray(keep, F32)[:, None]
        tabs += [mask * pr[dist - 1][None, :], mask * pi[dist - 1][None, :]]
    order = [SUBLANES - 1 - i for i in range(SUBLANES)] if reverse else list(range(SUBLANES))
    tabs += [jnp.stack([pr[i] for i in order]), jnp.stack([pi[i] for i in order])]
    return {"br": bdiag_in(bx_re).astype(BF16), "bi": bdiag_in(bx_im).astype(BF16),
            "cr": bdiag_out(c_re).astype(BF16), "ci": bdiag_out(-c_im).astype(BF16),
            "tab": jnp.stack(tabs)}


def _gelu(x):
    return x * (0.5 * (1.0 + jnp.tanh(math.sqrt(2.0 / math.pi) * (x + 0.044715 * (x * x * x)))))


def _evout_kernel(x_ref, att_ref, y_ref, gt_ref, gw_ref, gb_ref, wa_ref, ws_ref, o_ref):
    g = _gelu(y_ref[...])
    z = _bdot(g.astype(BF16), gw_ref[...]) + gb_ref[...]
    ssm = g * jax.nn.sigmoid(z)
    att = jnp.concatenate([att_ref[h] for h in range(MLA_HEADS)], axis=1)
    o = _bdot(att.astype(BF16), wa_ref[...]) + _bdot(ssm.astype(BF16), ws_ref[...])
    o_ref[...] = x_ref[...] + gt_ref[...] * o


def _even_out(x, att, y, gate, glu_w, glu_b, w_out):
    n, d = x.shape
    tm = min(512, n)
    wa = w_out[:MLA_HEADS * V_DIM].astype(BF16)
    ws = w_out[MLA_HEADS * V_DIM:].astype(BF16)
    gw = glu_w.astype(BF16)
    gb = glu_b.reshape(1, S5_WIDTH)
    full = lambda a: pl.BlockSpec(a.shape, lambda i: (0,) * a.ndim)
    return pl.pallas_call(
        _evout_kernel,
        grid=(n // tm,),
        in_specs=[pl.BlockSpec((tm, d), lambda i: (i, 0)),
                  pl.BlockSpec((MLA_HEADS, tm, V_DIM), lambda i: (0, i, 0)),
                  pl.BlockSpec((tm, S5_WIDTH), lambda i: (i, 0)),
                  full(gate), full(gw), full(gb), full(wa), full(ws)],
        out_specs=pl.BlockSpec((tm, d), lambda i: (i, 0)),
        out_shape=jax.ShapeDtypeStruct((n, d), F32),
        compiler_params=_cparams("parallel"),
        name="even_out",
    )(x, att, y, gate, gw, gb, wa, ws)


def _resid_kernel(x_ref, y_ref, gt_ref, w_ref, o_ref):
    o_ref[...] = x_ref[...] + gt_ref[...] * _bdot(y_ref[...].astype(BF16), w_ref[...])


def _resid_proj(x, y, gate, w):
    n, d = x.shape
    tm = min(512, n)
    wb = w.astype(BF16)
    full = lambda a: pl.BlockSpec(a.shape, lambda i: (0,) * a.ndim)
    return pl.pallas_call(
        _resid_kernel,
        grid=(n // tm,),
        in_specs=[pl.BlockSpec((tm, d), lambda i: (i, 0)), pl.BlockSpec((tm, y.shape[1]), lambda i: (i, 0)),
                  full(gate), full(wb)],
        out_specs=pl.BlockSpec((tm, d), lambda i: (i, 0)),
        out_shape=jax.ShapeDtypeStruct((n, d), F32),
        compiler_params=_cparams("parallel"),
        name="resid_proj",
    )(x, y, gate, wb)


def _sconv_kernel(z_ref, zp_ref, zn_ref, w_ref, b_ref, o_ref):
    i = pl.program_id(0)
    z = z_ref[...]
    tm = z.shape[0]
    rows = lax.broadcasted_iota(jnp.int32, z.shape, 0)
    prev_row = jnp.where(i > 0, zp_ref[SUBLANES - 1:SUBLANES, :], 0.0)
    next_row = jnp.where(i < pl.num_programs(0) - 1, zn_ref[0:1, :], 0.0)
    z_dn = jnp.where(rows == 0, prev_row, pltpu.roll(z, 1, 0))
    z_up = jnp.where(rows == tm - 1, next_row, pltpu.roll(z, tm - 1, 0))
    o_ref[...] = w_ref[0:1, :] * z_dn + w_ref[1:2, :] * z + w_ref[2:3, :] * z_up + b_ref[...]


def _short_conv(z, conv_w, conv_b):
    n, ch = z.shape
    tm = min(512, n)
    tc = 2048
    per = tm // SUBLANES
    last = n // SUBLANES - 1
    return pl.pallas_call(
        _sconv_kernel,
        grid=(n // tm, ch // tc),
        in_specs=[pl.BlockSpec((tm, tc), lambda i, j: (i, j)),
                  pl.BlockSpec((SUBLANES, tc), lambda i, j: (jnp.maximum(i * per - 1, 0), j)),
                  pl.BlockSpec((SUBLANES, tc), lambda i, j: (jnp.minimum((i + 1) * per, last), j)),
                  pl.BlockSpec((conv_w.shape[0], tc), lambda i, j: (0, j)),
                  pl.BlockSpec((1, tc), lambda i, j: (0, j))],
        out_specs=pl.BlockSpec((tm, tc), lambda i, j: (i, j)),
        out_shape=jax.ShapeDtypeStruct((n, ch), F32),
        compiler_params=_cparams("parallel", "parallel"),
        name="short_conv",
    )(z, z, z, conv_w, conv_b.reshape(1, ch))


def _filt_kernel(bands_ref, w1t_ref, w1c_ref, w1s_ref, b1_ref, w2_ref, b2_ref, w3_ref, b3_ref, fr_ref,
                 woh_ref, wol_ref, dl_ref, g_ref, s_ref, *, seq):
    i = pl.program_id(0)
    tl = g_ref.shape[1]
    rows = tl // FILT_PACK

    def lag_of(m):
        return jnp.where(m < seq, m, 2 * seq - m).astype(F32) / seq

    def packed_t(lanes_per_group):
        shape = (rows, FILT_PACK * lanes_per_group)
        r = lax.broadcasted_iota(jnp.int32, shape, 0)
        grp = lax.shift_right_logical(lax.broadcasted_iota(jnp.int32, shape, 1), int(math.log2(lanes_per_group)))
        return lag_of(i * tl + grp * rows + r)

    ang = ((2.0 * math.pi) * packed_t(HY_BANDS)) * bands_ref[...]
    fr = fr_ref[...]
    pre = (packed_t(HY_FILT_HIDDEN) * w1t_ref[...] + _hdot(jnp.cos(ang), w1c_ref[...])
           + _hdot(-jnp.sin(ang), w1s_ref[...]) + b1_ref[...])
    h = jnp.sin(fr * pre)
    h = jnp.sin(fr * (_hdot(h, w2_ref[...]) + b2_ref[...]))
    h = jnp.sin(fr * (_hdot(h, w3_ref[...]) + b3_ref[...]))
    lane = lax.broadcasted_iota(jnp.int32, (rows, LANES), 1)
    pieces = []
    for gi in range(FILT_PACK):
        slab = h[:, (gi // 2) * LANES:(gi // 2 + 1) * LANES]
        if gi % 2:
            slab = pltpu.roll(slab, HY_FILT_HIDDEN, 1)
        pieces.append(jnp.where(lane < HY_FILT_HIDDEN, slab, 0.0))
    hid = jnp.concatenate(pieces, axis=0)
    m = i * tl + lax.broadcasted_iota(jnp.int32, (tl, 1), 0)
    window = jnp.where(m == seq, 0.0, jnp.exp(-lag_of(m) * dl_ref[...]))

    @pl.when(i == 0)
    def _():
        s_ref[...] = jnp.zeros_like(s_ref)

    hid = _split(hid)
    for o in range(g_ref.shape[0]):
        f = _dot3(hid, (woh_ref[o], wol_ref[o])) * window
        g_ref[o] = f
        s_ref[o] += jnp.sum(jnp.abs(f), axis=0, keepdims=True)


def _hyena_filter(seq, w1, b1, w2, b2, w3, b3, freq, w_out):
    fh = HY_FILT_HIDDEN
    orders = w_out.shape[1]
    w = HY_WIDTH
    tl = min(512, seq)
    assert tl % (FILT_PACK * SUBLANES) == 0 and FILT_PACK * HY_BANDS == LANES
    eye = jnp.eye(FILT_PACK, dtype=F32)
    pad2 = lambda a: jnp.kron(eye, a)
    padv = lambda a: jnp.tile(a.reshape(1, -1), (1, FILT_PACK))
    bands = padv(jnp.linspace(1e-4, HY_BANDS - 1, HY_BANDS, dtype=F32))
    w1t = padv(w1[0])
    w1c = pad2(w1[1:1 + HY_BANDS])
    w1s = pad2(w1[1 + HY_BANDS:])
    wo_hi, wo_lo = _split(jnp.pad(w_out.transpose(1, 2, 0, 3),
                                  ((0, 0), (0, 0), (0, LANES - fh), (0, 0))))
    deltas = jnp.abs(jnp.linspace(HY_DECAY_MIN, HY_DECAY_MAX, w, dtype=F32)).reshape(1, w)
    nt = 2 * seq // tl
    half = seq // tl
    full = lambda a: pl.BlockSpec(a.shape, lambda i: (0,) * a.ndim)
    args = (bands, w1t, w1c, w1s, padv(b1), pad2(w2), padv(b2), pad2(w3), padv(b3), padv(freq))
    return pl.pallas_call(
        functools.partial(_filt_kernel, seq=seq),
        grid=(nt,),
        in_specs=[full(a) for a in args] + [pl.BlockSpec((orders, None, LANES, w), lambda i: (0, i // half, 0, 0)),
                                            pl.BlockSpec((orders, None, LANES, w), lambda i: (0, i // half, 0, 0)),
                                            full(deltas)],
        out_specs=[pl.BlockSpec((orders, tl, w), lambda i: (0, i, 0)),
                   pl.BlockSpec((orders, 1, w), lambda i: (0, 0, 0))],
        out_shape=[jax.ShapeDtypeStruct((orders, 2 * seq, w), F32), jax.ShapeDtypeStruct((orders, 1, w), F32)],
        compiler_params=_cparams("arbitrary"),
        name="hyena_filter",
    )(*args, wo_hi, wo_lo, deltas)


def _dft_consts(seq):
    n = 2 * seq
    n2 = DFT_N2
    n1 = n // n2
    k1 = n1 // 2 + 1
    two_pi = 2.0 * np.pi
    kk, nn = np.meshgrid(np.arange(k1), np.arange(n1), indexing="ij")
    ang_a = two_pi * ((kk * nn) % n1) / n1
    fa = (np.cos(ang_a), -np.sin(ang_a))
    k2, m2 = np.meshgrid(np.arange(n2), np.arange(n2), indexing="ij")
    ang_c = two_pi * ((k2 * m2) % n2) / n2
    fc = (np.cos(ang_c), -np.sin(ang_c))
    kk, mm = np.meshgrid(np.arange(k1), np.arange(n2), indexing="ij")
    ang_t = two_pi * ((kk * mm) % n) / n
    tw = (np.cos(ang_t), -np.sin(ang_t))
    weight = np.where((np.arange(k1) == 0) | (np.arange(k1) == n1 // 2), 1.0, 2.0) / n
    rows, cols = np.meshgrid(np.arange(n1 // 2), np.arange(k1), indexing="ij")
    ang_f = two_pi * ((rows * cols) % n1) / n1
    mc = np.cos(ang_f) * weight[None, :]
    ms = -np.sin(ang_f) * weight[None, :]
    kp = -(-k1 // SUBLANES) * SUBLANES
    rpad = lambda a: np.pad(a, ((0, kp - k1), (0, 0)))
    cpad = lambda a: np.pad(a, ((0, 0), (0, kp - k1)))
    f32 = lambda a: jnp.asarray(a, F32)
    fc3 = jnp.stack([_lhs3(f32(m)) for m in (fc[0], fc[1] - fc[0], fc[0] + fc[1])])
    mcs = jnp.concatenate([f32(cpad(mc)), f32(cpad(ms))], axis=1)
    return {"n1": n1, "k1": k1, "kp": kp,
            "fa_r": f32(rpad(fa[0])), "fa_i": f32(rpad(fa[1])), "fc3": fc3,
            "twa_r": f32(rpad(tw[0]).T[:, :, None]), "twa_i": f32(rpad(tw[1]).T[:, :, None]),
            "twc_r": f32(rpad(tw[0])[:, :, None]), "twc_i": f32(rpad(tw[1])[:, :, None]),
            "mcs3": _lhs3(mcs)}


def _dfta_kernel(x_ref, f_ref, twr_ref, twi_ref, or_ref, oi_ref):
    kp = or_ref.shape[0]
    f = f_ref[...]
    xt = pltpu.einshape("nsw->snw", x_ref[...])
    out_r, out_i = [], []
    for s in range(SUBLANES):
        a = _bdot(f, _rhs3(xt[s]))
        ar, ai = a[:kp], a[kp:]
        tr = twr_ref[s]
        ti = twi_ref[s]
        out_r.append(ar * tr - ai * ti)
        out_i.append(ar * ti + ai * tr)
    or_ref[...] = pltpu.einshape("skw->ksw", jnp.stack(out_r))
    oi_ref[...] = pltpu.einshape("skw->ksw", jnp.stack(out_i))


def _dft_stage_a(x3, col0, dc, n1c, row_blk=0):
    k1 = dc["kp"]
    tw = 1024
    wt = HY_WIDTH // tw
    f3 = _lhs3(jnp.concatenate([dc["fa_r"][:, :n1c], dc["fa_i"][:, :n1c]], axis=0))
    out = jax.ShapeDtypeStruct((k1, DFT_N2, HY_WIDTH), F32)
    return pl.pallas_call(
        _dfta_kernel,
        grid=(DFT_N2 // SUBLANES, wt),
        in_specs=[pl.BlockSpec((n1c, SUBLANES, tw), lambda b, j: (row_blk, b, col0 * wt + j)),
                  pl.BlockSpec(f3.shape, lambda b, j: (0, 0)),
                  pl.BlockSpec((SUBLANES, k1, 1), lambda b, j: (b, 0, 0)),
                  pl.BlockSpec((SUBLANES, k1, 1), lambda b, j: (b, 0, 0))],
        out_specs=[pl.BlockSpec((k1, SUBLANES, tw), lambda b, j: (0, b, j))] * 2,
        out_shape=[out, out],
        compiler_params=_cparams("parallel", "parallel"),
        name="dft_stage_a",
    )(x3, f3, dc["twa_r"], dc["twa_i"])


def _cmul3(f_ref, ar, ai):
    k1 = _bdot(f_ref[0], _rhs3(ar + ai))
    return k1 - _bdot(f_ref[2], _rhs3(ai)), k1 + _bdot(f_ref[1], _rhs3(ar))


def _dftc_filt_kernel(ar_ref, ai_ref, f_ref, s_ref, gr_ref, gi_ref, *, k1):
    @pl.when(pl.program_id(0) < k1)
    def _():
        xr, xi = _cmul3(f_ref, ar_ref[...], ai_ref[...])
        inv = 1.0 / s_ref[...]
        gr_ref[...] = xr * inv
        gi_ref[...] = xi * inv

    @pl.when(pl.program_id(0) >= k1)
    def _():
        gr_ref[...] = jnp.zeros_like(gr_ref)
        gi_ref[...] = jnp.zeros_like(gi_ref)


def _dft_filter_spectrum(ar, ai, abs_sum, dc):
    kp = dc["kp"]
    tw = 1024
    blk = pl.BlockSpec((None, DFT_N2, tw), lambda k, j: (k, 0, j))
    mat = pl.BlockSpec(dc["fc3"].shape, lambda k, j: (0, 0, 0))
    out = jax.ShapeDtypeStruct((kp, DFT_N2, HY_WIDTH), F32)
    return pl.pallas_call(
        functools.partial(_dftc_filt_kernel, k1=dc["k1"]),
        grid=(kp, HY_WIDTH // tw),
        in_specs=[blk, blk, mat, pl.BlockSpec((1, tw), lambda k, j: (0, j))],
        out_specs=[blk, blk],
        out_shape=[out, out],
        compiler_params=_cparams("parallel", "parallel"),
        name="dft_filter_spectrum",
    )(ar, ai, dc["fc3"], abs_sum)


def _dftmid_kernel(ar_ref, ai_ref, gr_ref, gi_ref, f_ref, twr_ref, twi_ref, or_ref, oi_ref, *, k1):
    @pl.when(pl.program_id(0) < k1)
    def _():
        xr, xi = _cmul3(f_ref, ar_ref[...], ai_ref[...])
        gr = gr_ref[...]
        gi = gi_ref[...]
        yr = xr * gr - xi * gi
        yi = xr * gi + xi * gr
        k1_ = _bdot(f_ref[0], _rhs3(yr + yi))
        br = k1_ + _bdot(f_ref[1], _rhs3(yi))
        bi = k1_ - _bdot(f_ref[2], _rhs3(yr))
        tr = twr_ref[...]
        ti = twi_ref[...]
        or_ref[...] = br * tr + bi * ti
        oi_ref[...] = bi * tr - br * ti

    @pl.when(pl.program_id(0) >= k1)
    def _():
        or_ref[...] = jnp.zeros_like(or_ref)
        oi_ref[...] = jnp.zeros_like(oi_ref)


def _dft_middle(ar, ai, gr, gi, dc):
    kp = dc["kp"]
    tw = 1024
    blk = pl.BlockSpec((None, DFT_N2, tw), lambda k, j: (k, 0, j))
    mat = pl.BlockSpec(dc["fc3"].shape, lambda k, j: (0, 0, 0))
    twd = pl.BlockSpec((None, DFT_N2, 1), lambda k, j: (k, 0, 0))
    out = jax.ShapeDtypeStruct((kp, DFT_N2, HY_WIDTH), F32)
    return pl.pallas_call(
        functools.partial(_dftmid_kernel, k1=dc["k1"]),
        grid=(kp, HY_WIDTH // tw),
        in_specs=[blk, blk, blk, blk, mat, twd, twd],
        out_specs=[blk, blk],
        out_shape=[out, out],
        compiler_params=_cparams("parallel", "parallel"),
        name="dft_middle",
    )(ar, ai, gr, gi, dc["fc3"], dc["twc_r"], dc["twc_i"])


def _dftf_kernel(br_ref, bi_ref, m_ref, gate_ref, y_ref, skip_ref, o_ref):
    m = m_ref[...]
    brt = pltpu.einshape("ksw->skw", br_ref[...])
    bit = pltpu.einshape("ksw->skw", bi_ref[...])
    conv = [_bdot(m, _rhs3(jnp.concatenate([brt[s], bit[s]], axis=0))) for s in range(SUBLANES)]
    conv = pltpu.einshape("snw->nsw", jnp.stack(conv))
    o_ref[...] = gate_ref[...] * (conv + skip_ref[...] * y_ref[...])


def _dft_final(br, bi, dc, gate3, gate_col, y3, y_col, skip):
    k1 = dc["kp"]
    nh = dc["n1"] // 2
    tw = 1024
    wt = HY_WIDTH // tw
    bblk = pl.BlockSpec((k1, SUBLANES, tw), lambda b, j: (0, b, j))
    mat = pl.BlockSpec(dc["mcs3"].shape, lambda b, j: (0, 0))
    return pl.pallas_call(
        _dftf_kernel,
        grid=(DFT_N2 // SUBLANES, wt),
        in_specs=[bblk, bblk, mat,
                  pl.BlockSpec((nh, SUBLANES, tw), lambda b, j: (0, b, gate_col * wt + j)),
                  pl.BlockSpec((nh, SUBLANES, tw), lambda b, j: (0, b, y_col * wt + j)),
                  pl.BlockSpec((1, tw), lambda b, j: (0, j))],
        out_specs=pl.BlockSpec((nh, SUBLANES, tw), lambda b, j: (0, b, j)),
        out_shape=jax.ShapeDtypeStruct((nh, DFT_N2, HY_WIDTH), F32),
        compiler_params=_cparams("parallel", "parallel"),
        name="dft_final",
    )(br, bi, dc["mcs3"], gate3, y3, skip)


def _hyena_mixer(x, shift, scale, gate, g_norm, w_in, conv_w, conv_b, w1, b1, w2, b2, w3, b3, freq,
                 filt_w_out, skip, w_out):
    seq = x.shape[0]
    dc = _dft_consts(seq)
    nh = dc["n1"] // 2
    z = _norm_proj(x, shift, scale, g_norm, w_in)
    zc = _short_conv(z, conv_w, conv_b)
    zc3 = zc.reshape(nh, DFT_N2, 3 * HY_WIDTH)
    y3, y_col = zc3, 2
    g_all, abs_sums = _hyena_filter(seq, w1, b1, w2, b2, w3, b3, freq, filt_w_out)
    for o in range(filt_w_out.shape[1]):
        ga_r, ga_i = _dft_stage_a(g_all.reshape(-1, DFT_N2, HY_WIDTH), 0, dc, dc["n1"], row_blk=o)
        gr, gi = _dft_filter_spectrum(ga_r, ga_i, abs_sums[o], dc)
        ya_r, ya_i = _dft_stage_a(y3, y_col, dc, nh)
        br, bi = _dft_middle(ya_r, ya_i, gr, gi, dc)
        y3 = _dft_final(br, bi, dc, zc3, o, y3, y_col, skip[o].reshape(1, HY_WIDTH))
        y_col = 0
    return _resid_proj(x, y3.reshape(seq, HY_WIDTH), gate, w_out)


def _even_mixer(x, cx, mx, mc, g_norm, ev_w_in, q_a_norm_g, w_uq, kv_a_norm_g, w_ukv, q_head_g, k_head_g,
                lam_re, lam_im, log_dt, b_re, b_im, c_re, c_im, d_skip, glu_w, glu_b, ev_w_out):
    n = x.shape[0]
    nc = cx.shape[0]
    cuts = (Q_LORA, Q_LORA + KV_LORA, Q_LORA + KV_LORA + QK_ROPE)
    w_qa, w_kva, w_kpe, w_u = (ev_w_in[:, :cuts[0]], ev_w_in[:, cuts[0]:cuts[1]],
                               ev_w_in[:, cuts[1]:cuts[2]], ev_w_in[:, cuts[2]:])
    w_all = jnp.concatenate([w_u, w_qa, w_kva, _pad_lanes(w_kpe), _pad_lanes(w_kpe[:, _ROPE_SWAP])], axis=1)
    proj_x = _norm_proj(x, mx[3], mx[4], g_norm, w_all)
    proj_c = _norm_proj(cx, mc[3], mc[4], g_norm, w_all)

    wq3, wkv, gq3, gk3 = _mla_weights(w_uq, w_ukv, q_head_g, k_head_g)
    gqa = q_a_norm_g.reshape(1, Q_LORA)
    gkva = kv_a_norm_g.reshape(1, KV_LORA)
    cos_x, sin_x = _rope_tables(n)
    cos_c = _pad_lanes(jnp.ones((nc, QK_ROPE), F32))
    sin_c = jnp.zeros((nc, LANES), F32)
    q_x, k_x, v_x = _mla_heads(proj_x, cos_x, sin_x, gqa, gkva, wq3, wkv, gq3, gk3)
    _, k_c, v_c = _mla_heads(proj_c, cos_c, sin_c, gqa, gkva, wq3, wkv, gq3, gk3)
    att = _attention(q_x, jnp.concatenate([k_x, k_c], axis=1), jnp.concatenate([v_x, v_c], axis=1))

    d_row = d_skip.reshape(1, S5_WIDTH)
    zero = jnp.zeros((SUBLANES, S5_NSTATE), F32)
    y = None
    for direction, reverse in ((0, False), (1, True)):
        prm = _s5_params(lam_re[direction], lam_im[direction], log_dt[direction], b_re[direction],
                         b_im[direction], c_re[direction], c_im[direction], reverse)
        _, hc_r, hc_i = _s5_scan(proj_c, d_row, (zero, zero), prm, reverse=reverse, first=True)
        if y is None:
            y, _, _ = _s5_scan(proj_x, d_row, (hc_r, hc_i), prm, reverse=reverse, first=True)
        else:
            y, _, _ = _s5_scan(proj_x, y, (hc_r, hc_i), prm, reverse=reverse, first=False)
    return _even_out(x, att, y, mx[5], glu_w, glu_b, ev_w_out)


def kernel(x, c, ctx, c_ctx, ada_w, ada_b, norm_g, ffn_w_in, ffn_w_out, ev_w_in, mla_q_a_norm_g, mla_w_uq, mla_kv_a_norm_g, mla_w_ukv, mla_q_head_g, mla_k_head_g, s5_lam_re, s5_lam_im, s5_log_dt, s5_b_re, s5_b_im, s5_c_re, s5_c_im, s5_d, s5_glu_w, s5_glu_b, ev_w_out, hy_w_in, hy_conv_w, hy_conv_b, hy_filt_w1, hy_filt_b1, hy_filt_w2, hy_filt_b2, hy_filt_w3, hy_filt_b3, hy_filt_freq, hy_filt_w_out, hy_skip, hy_w_out):
    assert x.shape[0] == 1 and ctx.shape[0] == 1, "kernels are written for batch 1"
    depth = ada_w.shape[0]
    mods = _ada_mods(c, c_ctx, ada_w, ada_b)
    xs = x[0]
    cx = ctx[0]
    for i in range(depth):
        even = i % 2 == 0
        li = i // 2
        need_after = any(j % 2 == 0 for j in range(i + 1, depth))
        use_ctx = even or need_after
        assert not need_after, "context outputs after an even layer are not needed at this depth"
        mx = mods[i, 0]
        mc = mods[i, 1]
        xs = _ffn_half(xs, mx, 0, norm_g[i, 0], ffn_w_in[i, 0], ffn_w_out[i, 0])
        if use_ctx:
            cx = _ffn_half(cx, mc, 0, norm_g[i, 0], ffn_w_in[i, 0], ffn_w_out[i, 0])
        if even:
            xs = _even_mixer(xs, cx, mx, mc, norm_g[i, 1], ev_w_in[li], mla_q_a_norm_g[li], mla_w_uq[li],
                             mla_kv_a_norm_g[li], mla_w_ukv[li], mla_q_head_g[li], mla_k_head_g[li],
                             s5_lam_re[li], s5_lam_im[li], s5_log_dt[li], s5_b_re[li], s5_b_im[li],
                             s5_c_re[li], s5_c_im[li], s5_d[li], s5_glu_w[li], s5_glu_b[li], ev_w_out[li])
        else:
            xs = _hyena_mixer(xs, mx[3], mx[4], mx[5], norm_g[i, 1], hy_w_in[li], hy_conv_w[li], hy_conv_b[li],
                              hy_filt_w1[li], hy_filt_b1[li], hy_filt_w2[li], hy_filt_b2[li], hy_filt_w3[li],
                              hy_filt_b3[li], hy_filt_freq[li], hy_filt_w_out[li], hy_skip[li], hy_w_out[li])
        xs = _ffn_half(xs, mx, 6, norm_g[i, 2], ffn_w_in[i, 1], ffn_w_out[i, 1])
    return xs[None]
```
